```python
import jax, jax.numpy as jnp
from jax import lax
import numpy as np

D_MODEL = 2048
BATCH = 8
SEQ = 4096
DEPTH = 2

EPS = 1e-6
N_BRANCH = 4
D_FF = 4 * D_MODEL
POOL_DIM = 512
POOL_WINDOWS = (2, 4, 8, 16)
POOL_GROUPS = len(POOL_WINDOWS)
POOL_GDIM = POOL_DIM // POOL_GROUPS
CONV_DIM = 512
CONV_WIDTH = 31
SGU_DIM = 512
SGU_GROUPS = 4
SGU_GDIM = SGU_DIM // SGU_GROUPS
CHUNK = 128
MLA_HEADS = 8
Q_LORA = 512
KV_LORA = 512
QK_NOPE = 128
QK_ROPE = 64
V_DIM = 128
ROPE_THETA = 10000.0
ATTN_BLOCK = 128
OFF_POOL = 0
OFF_CONV = OFF_POOL + POOL_DIM
OFF_SGU = OFF_CONV + 2 * CONV_DIM
OFF_Q = OFF_SGU + 2 * SGU_DIM
OFF_KV = OFF_Q + Q_LORA
OFF_KR = OFF_KV + KV_LORA
OFF_GATE = OFF_KR + QK_ROPE
N_IN = OFF_GATE + N_BRANCH * D_MODEL

kernel_name = 'hybrid_gated_pool_conv_sgu_mla_block'


def rmsnorm(x, g):
    xf = x.astype(jnp.float32)
    y = xf * lax.rsqrt(jnp.mean(xf * xf, axis=-1, keepdims=True) + EPS)
    return (y * g.astype(jnp.float32)).astype(x.dtype)


def layernorm(x, g, b):
    xf = x.astype(jnp.float32)
    mu = jnp.mean(xf, axis=-1, keepdims=True)
    var = jnp.mean(jnp.square(xf - mu), axis=-1, keepdims=True)
    y = (xf - mu) * lax.rsqrt(var + EPS)
    return (y * g.astype(jnp.float32) + b.astype(jnp.float32)).astype(x.dtype)


def pool_mixer(a, pool_w, pool_scale):
    B, S, _ = a.shape
    af = a.astype(jnp.float32)
    csum = jnp.cumsum(af, axis=1)
    t = jnp.arange(S)
    means = []
    for gi, w in enumerate(POOL_WINDOWS):
        cs = csum[..., gi * POOL_GDIM:(gi + 1) * POOL_GDIM]
        lagged = jnp.pad(cs, ((0, 0), (w, 0), (0, 0)))[:, :S]
        count = jnp.minimum(t + 1, w).astype(jnp.float32)
        means.append((cs - lagged) / count[None, :, None])
    pooled = (jnp.concatenate(means, axis=-1) - af).astype(a.dtype)
    pooled = pooled.reshape(B, S, POOL_GROUPS, POOL_GDIM)
    mixed = jnp.einsum('bsgc,gcd->bsgd', pooled, pool_w).reshape(B, S, POOL_DIM)
    return mixed * pool_scale


def conformer_conv(c, conv_w, conv_b, norm_g, norm_b):
    a, gate = jnp.split(c, 2, axis=-1)
    glu = a * jax.nn.sigmoid(gate)
    padded = jnp.pad(glu, ((0, 0), (CONV_WIDTH - 1, 0), (0, 0)))
    y = lax.conv_general_dilated(
        padded, conv_w[:, None, :], window_strides=(1,), padding='VALID',
        dimension_numbers=('NWC', 'WIO', 'NWC'), feature_group_count=CONV_DIM)
    y = layernorm(y + conv_b, norm_g, norm_b)
    return jax.nn.silu(y)


def spatial_gating(z, norm_g, norm_b, w_s, b_s):
    z = jax.nn.gelu(z)
    u, v = jnp.split(z, 2, axis=-1)
    v = layernorm(v, norm_g, norm_b)
    B, S, _ = v.shape
    v = v.reshape(B, S // CHUNK, CHUNK, SGU_GROUPS, SGU_GDIM)
    mask = jnp.tril(jnp.ones((CHUNK, CHUNK), dtype=bool))
    w = jnp.where(mask[None], w_s, 0)
    sp = jnp.einsum('gts,bnsgc->bntgc', w, v) + b_s.T[None, None, :, :, None]
    return u * sp.reshape(B, S, SGU_DIM)


def apply_rope(x, cos, sin):
    x1, x2 = jnp.split(x.astype(jnp.float32), 2, axis=-1)
    return jnp.concatenate([x1 * cos - x2 * sin, x2 * cos + x1 * sin], axis=-1).astype(x.dtype)


def latent_attention(cq, ckv, kr, cos, sin, q_norm_g, w_uq, kv_norm_g, w_ukv, attn_proj):
    B, S, _ = cq.shape
    q = (rmsnorm(cq, q_norm_g) @ w_uq).reshape(B, S, MLA_HEADS, QK_NOPE + QK_ROPE)
    q_nope = q[..., :QK_NOPE]
    q_rope = apply_rope(q[..., QK_NOPE:], cos[:, :, None], sin[:, :, None])
    kv = (rmsnorm(ckv, kv_norm_g) @ w_ukv).reshape(B, S, MLA_HEADS, QK_NOPE + V_DIM)
    k_nope, v = kv[..., :QK_NOPE], kv[..., QK_NOPE:]
    k_rope = apply_rope(kr, cos, sin)
    scale = (QK_NOPE + QK_ROPE) ** -0.5
    outs = []
    for i in range(S // ATTN_BLOCK):
        q0, q1 = i * ATTN_BLOCK, (i + 1) * ATTN_BLOCK
        s = (jnp.einsum('bqhd,bkhd->bhqk', q_nope[:, q0:q1], k_nope[:, :q1])
             + jnp.einsum('bqhd,bkd->bhqk', q_rope[:, q0:q1], k_rope[:, :q1]))
        s = s.astype(jnp.float32) * scale
        mask = jnp.arange(q1)[None, :] <= jnp.arange(q0, q1)[:, None]
        s = jnp.where(mask, s, jnp.finfo(jnp.float32).min)
        p = jax.nn.softmax(s, axis=-1).astype(v.dtype)
        outs.append(jnp.einsum('bhqk,bkhd->bqhd', p, v[:, :q1]))
    o = jnp.concatenate(outs, axis=1).reshape(B, S, MLA_HEADS * V_DIM)
    return o @ attn_proj


def _normal(key, shape, scale):
    return jax.random.normal(key, shape, jnp.float32) * scale


def _gain(key, n):
    return 1.0 + 0.02 * jax.random.normal(key, (DEPTH, n), jnp.float32)


def _bias(key, n):
    return 0.02 * jax.random.normal(key, (DEPTH, n), jnp.float32)


def setup_inputs(seed: int = 0) -> dict:
    key = jax.random.key(seed)
    ks = jax.random.split(key, 27)
    L = DEPTH
    return {
        'x': _normal(ks[0], (BATCH, SEQ, D_MODEL), 1.0),
        'positions': jnp.broadcast_to(jnp.arange(SEQ, dtype=jnp.int32)[None, :], (BATCH, SEQ)),
        'pre_mix_g': _gain(ks[1], D_MODEL),
        'w_in': _normal(ks[2], (L, D_MODEL, N_IN), D_MODEL ** -0.5),
        'pool_w': _normal(ks[3], (L, POOL_GROUPS, POOL_GDIM, POOL_GDIM), POOL_GDIM ** -0.5),
        'pool_scale': _gain(ks[4], POOL_DIM),
        'pool_proj': _normal(ks[5], (L, POOL_DIM, D_MODEL), POOL_DIM ** -0.5),
        'conv_w': _normal(ks[6], (L, CONV_WIDTH, CONV_DIM), CONV_WIDTH ** -0.5),
        'conv_b': _bias(ks[7], CONV_DIM),
        'conv_norm_g': _gain(ks[8], CONV_DIM),
        'conv_norm_b': _bias(ks[9], CONV_DIM),
        'conv_proj': _normal(ks[10], (L, CONV_DIM, D_MODEL), CONV_DIM ** -0.5),
        'sgu_norm_g': _gain(ks[11], SGU_DIM),
        'sgu_norm_b': _bias(ks[12], SGU_DIM),
        'sgu_w': _normal(ks[13], (L, SGU_GROUPS, CHUNK, CHUNK), CHUNK ** -0.5),
        'sgu_b': 1.0 + 0.02 * jax.random.normal(ks[14], (L, SGU_GROUPS, CHUNK), jnp.float32),
        'sgu_proj': _normal(ks[15], (L, SGU_DIM, D_MODEL), SGU_DIM ** -0.5),
        'q_norm_g': _gain(ks[16], Q_LORA),
        'w_uq': _normal(ks[17], (L, Q_LORA, MLA_HEADS * (QK_NOPE + QK_ROPE)), Q_LORA ** -0.5),
        'kv_norm_g': _gain(ks[18], KV_LORA),
        'w_ukv': _normal(ks[19], (L, KV_LORA, MLA_HEADS * (QK_NOPE + V_DIM)), KV_LORA ** -0.5),
        'attn_proj': _normal(ks[20], (L, MLA_HEADS * V_DIM, D_MODEL), (MLA_HEADS * V_DIM) ** -0.5),
        'w_out': _normal(ks[21], (L, D_MODEL, D_MODEL), D_MODEL ** -0.5),
        'post_mix_g': _gain(ks[22], D_MODEL),
        'pre_mlp_g': _gain(ks[23], D_MODEL),
        'w_up': _normal(ks[24], (L, D_MODEL, D_FF), D_MODEL ** -0.5),
        'w_down': _normal(ks[25], (L, D_FF, D_MODEL), D_FF ** -0.5),
        'post_mlp_g': _gain(ks[26], D_MODEL),
    }


def reference(x, positions, pre_mix_g, w_in, pool_w, pool_scale, pool_proj, conv_w, conv_b,
              conv_norm_g, conv_norm_b, conv_proj, sgu_norm_g, sgu_norm_b, sgu_w, sgu_b,
              sgu_proj, q_norm_g, w_uq, kv_norm_g, w_ukv, attn_proj, w_out, post_mix_g,
              pre_mlp_g, w_up, w_down, post_mlp_g):
    B, S, _ = x.shape
    inv_freq = ROPE_THETA ** (-jnp.arange(0, QK_ROPE, 2, dtype=jnp.float32) / QK_ROPE)
    ang = positions.astype(jnp.float32)[..., None] * inv_freq
    cos, sin = jnp.cos(ang), jnp.sin(ang)
    for l in range(DEPTH):
        h = rmsnorm(x, pre_mix_g[l])
        z = h @ w_in[l]
        y_pool = pool_mixer(z[..., OFF_POOL:OFF_CONV], pool_w[l], pool_scale[l]) @ pool_proj[l]
        y_conv = conformer_conv(z[..., OFF_CONV:OFF_SGU], conv_w[l], conv_b[l],
                                conv_norm_g[l], conv_norm_b[l]) @ conv_proj[l]
        y_sgu = spatial_gating(z[..., OFF_SGU:OFF_Q], sgu_norm_g[l], sgu_norm_b[l],
                               sgu_w[l], sgu_b[l]) @ sgu_proj[l]
        y_attn = latent_attention(z[..., OFF_Q:OFF_KV], z[..., OFF_KV:OFF_KR],
                                  z[..., OFF_KR:OFF_GATE], cos, sin, q_norm_g[l], w_uq[l],
                                  kv_norm_g[l], w_ukv[l], attn_proj[l])
        gates = jax.nn.sigmoid(z[..., OFF_GATE:].reshape(B, S, N_BRANCH, D_MODEL))
        merged = (gates[:, :, 0] * y_pool + gates[:, :, 1] * y_conv
                  + gates[:, :, 2] * y_sgu + gates[:, :, 3] * y_attn)
        x = x + rmsnorm(merged @ w_out[l], post_mix_g[l])
        h = rmsnorm(x, pre_mlp_g[l])
        f = jnp.square(jax.nn.relu(h @ w_up[l])) @ w_down[l]
        x = x + rmsnorm(f, post_mlp_g[l])
    return x
```

```python
import functools

import jax
import jax.numpy as jnp
from jax import lax
from jax.experimental import pallas as pl
from jax.experimental.pallas import tpu as pltpu

F32 = jnp.float32
BF16 = jnp.bfloat16

D_MODEL = 2048
EPS = 1e-6
N_BRANCH = 4
D_FF = 4 * D_MODEL
POOL_DIM = 512
POOL_WINDOWS = (2, 4, 8, 16)
POOL_GDIM = POOL_DIM // len(POOL_WINDOWS)
CONV_DIM = 512
CONV_WIDTH = 31
SGU_DIM = 512
SGU_GROUPS = 4
SGU_GDIM = SGU_DIM // SGU_GROUPS
CHUNK = 128
MLA_HEADS = 8
Q_LORA = 512
KV_LORA = 512
QK_NOPE = 128
QK_ROPE = 64
V_DIM = 128
ROPE_THETA = 10000.0
OFF_KR = POOL_DIM + 2 * CONV_DIM + 2 * SGU_DIM + Q_LORA + KV_LORA
OFF_GATE = OFF_KR + QK_ROPE

LANES = 128
Q_HEAD_PAD = 2 * LANES
KR_PAD = LANES
N_SMALL = OFF_KR + KR_PAD
POOL_HALO = 16
CONV_HALO = 32
NEG_BIG = -1e30
VMEM_CAP = 56 * 1024 * 1024

_SEGS = (POOL_DIM, 2 * CONV_DIM, 2 * SGU_DIM, Q_LORA, KV_LORA, KR_PAD)


def _params(semantics, vmem_bytes):
    return pltpu.CompilerParams(dimension_semantics=semantics,
                                vmem_limit_bytes=int(min(VMEM_CAP, max(vmem_bytes, 16 * 1024 * 1024))))


def _resident(shape):
    nd = len(shape)
    return pl.BlockSpec(shape, lambda *_: (0,) * nd, pipeline_mode=pl.Buffered(1))


def _rms(x, g):
    return x * lax.rsqrt(jnp.mean(x * x, axis=-1, keepdims=True) + EPS) * g


def _layernorm(x, g, b):
    mu = jnp.mean(x, axis=-1, keepdims=True)
    xc = x - mu
    var = jnp.mean(xc * xc, axis=-1, keepdims=True)
    return xc * lax.rsqrt(var + EPS) * g + b


def _dot(a, b):
    return jnp.dot(a, b, preferred_element_type=F32)


def _rope_table_kernel(pos_ref, freq_ref, c_ref, s1_ref, s2_ref):
    ang = pos_ref[...].astype(F32) * freq_ref[...]
    lane = lax.broadcasted_iota(jnp.int32, ang.shape, 1)
    cos = jnp.cos(ang)
    sin = jnp.sin(ang)
    half = QK_ROPE // 2
    c_ref[...] = jnp.where(lane < QK_ROPE, cos, 0.0)
    s1_ref[...] = jnp.where(lane < half, 0.0, jnp.where(lane < QK_ROPE, sin, 0.0))
    s2_ref[...] = jnp.where(lane < half, -sin, 0.0)


def _rope_tables(pos, tm):
    m = pos.shape[0]
    inv_freq = ROPE_THETA ** (-jnp.arange(0, QK_ROPE, 2, dtype=F32) / QK_ROPE)
    freq = jnp.concatenate([inv_freq, inv_freq, jnp.zeros((LANES - QK_ROPE,), F32)])[None, :]
    out = jax.ShapeDtypeStruct((m, LANES), F32)
    spec = pl.BlockSpec((tm, LANES), lambda i: (i, 0))
    return pl.pallas_call(
        _rope_table_kernel,
        grid=(m // tm,),
        in_specs=[pl.BlockSpec((tm, 1), lambda i: (i, 0)), pl.BlockSpec((1, LANES), lambda i: (0, 0))],
        out_specs=[spec, spec, spec],
        out_shape=[out, out, out],
        compiler_params=_params(("parallel",), 0),
        name="rope_tables",
    )(pos, freq)


def _rope(chunk, c, s1, s2):
    half = QK_ROPE // 2
    return (chunk * c + pltpu.roll(chunk, half, 1) * s1
            + pltpu.roll(chunk, LANES - half, 1) * s2)


def _inproj_kernel(x_ref, g_ref, w_ref, h_ref, *z_refs):
    h = _rms(x_ref[...], g_ref[...]).astype(BF16)
    h_ref[...] = h
    off = 0
    for z_ref, width in zip(z_refs, _SEGS):
        for c in range(0, width, 512):
            cw = min(512, width - c)
            z_ref[:, c:c + cw] = _dot(h, w_ref[:, off + c:off + c + cw]).astype(BF16)
        off += width


def _inproj(x, g, w_small, tm):
    m = x.shape[0]
    row = lambda i: (i, 0)
    out_shape = [jax.ShapeDtypeStruct((m, D_MODEL), BF16)]
    out_specs = [pl.BlockSpec((tm, D_MODEL), row)]
    for width in _SEGS:
        out_shape.append(jax.ShapeDtypeStruct((m, width), BF16))
        out_specs.append(pl.BlockSpec((tm, width), row))
    vmem = (2 * tm * D_MODEL * 4 + 2 * tm * D_MODEL * 2 + D_MODEL * N_SMALL * 2
            + 2 * tm * N_SMALL * 2 + 4 * tm * D_MODEL * 4)
    return pl.pallas_call(
        _inproj_kernel,
        grid=(m // tm,),
        in_specs=[pl.BlockSpec((tm, D_MODEL), row), _resident((1, D_MODEL)),
                  _resident((D_MODEL, N_SMALL))],
        out_specs=out_specs,
        out_shape=out_shape,
        compiler_params=_params(("parallel",), vmem),
        name="inproj",
    )(x, g, w_small)


def _pool_kernel(a_ref, halo_ref, w_ref, scale_ref, o_ref, ext_ref, *, ts, tiles_per_seq):
    i = pl.program_id(0)
    first = (i % tiles_per_seq) == 0
    a = a_ref[...].astype(F32)
    ext_ref[0:POOL_HALO, :] = jnp.where(first, 0.0, halo_ref[...].astype(F32))
    ext_ref[POOL_HALO:, :] = a
    t = (i % tiles_per_seq) * ts + lax.broadcasted_iota(jnp.int32, (ts, 1), 0)
    for gi, w in enumerate(POOL_WINDOWS):
        c0, c1 = gi * POOL_GDIM, (gi + 1) * POOL_GDIM
        win = ext_ref[POOL_HALO:POOL_HALO + ts, c0:c1]
        for d in range(1, w):
            win = win + ext_ref[POOL_HALO - d:POOL_HALO - d + ts, c0:c1]
        count = jnp.minimum(t + 1, w).astype(F32)
        pooled = (win / count - a[:, c0:c1]).astype(BF16)
        o_ref[:, c0:c1] = (_dot(pooled, w_ref[gi]) * scale_ref[:, c0:c1]).astype(BF16)


def _pool(z_pool, pool_w, pool_scale, ts, seq):
    m = z_pool.shape[0]
    hb = ts // POOL_HALO
    kern = functools.partial(_pool_kernel, ts=ts, tiles_per_seq=seq // ts)
    return pl.pallas_call(
        kern,
        grid=(m // ts,),
        in_specs=[pl.BlockSpec((ts, POOL_DIM), lambda i: (i, 0)),
                  pl.BlockSpec((POOL_HALO, POOL_DIM), lambda i: (jnp.maximum(i * hb - 1, 0), 0)),
                  _resident(pool_w.shape), _resident((1, POOL_DIM))],
        out_specs=pl.BlockSpec((ts, POOL_DIM), lambda i: (i, 0)),
        out_shape=jax.ShapeDtypeStruct((m, POOL_DIM), BF16),
        scratch_shapes=[pltpu.VMEM((ts + POOL_HALO, POOL_DIM), F32)],
        compiler_params=_params(("parallel",), 0),
        name="pool",
    )(z_pool, z_pool, pool_w, pool_scale)


_CONV_ROWS = 64


def _glu(c):
    c = c.astype(F32)
    return c[:, :CONV_DIM] * jax.nn.sigmoid(c[:, CONV_DIM:])


def _conv_kernel(c_ref, halo_ref, w_ref, b_ref, ng_ref, nb_ref, o_ref, ext_ref, y_ref, *, ts, tiles_per_seq):
    i = pl.program_id(0)
    first = (i % tiles_per_seq) == 0
    ext_ref[0:CONV_HALO, :] = jnp.where(first, 0.0, _glu(halo_ref[...]))
    ext_ref[CONV_HALO:, :] = _glu(c_ref[...])
    base = CONV_HALO - (CONV_WIDTH - 1)
    for r in range(0, ts, _CONV_ROWS):
        for c in range(0, CONV_DIM, LANES):
            acc = None
            for k in range(CONV_WIDTH):
                term = ext_ref[base + r + k:base + r + k + _CONV_ROWS, c:c + LANES] * w_ref[k:k + 1, c:c + LANES]
                acc = term if acc is None else acc + term
            y_ref[r:r + _CONV_ROWS, c:c + LANES] = acc
    y = _layernorm(y_ref[...] + b_ref[...], ng_ref[...], nb_ref[...])
    o_ref[...] = (y * jax.nn.sigmoid(y)).astype(BF16)


def _conv(z_conv, conv_w, conv_b, norm_g, norm_b, ts, seq):
    m = z_conv.shape[0]
    hb = ts // CONV_HALO
    kern = functools.partial(_conv_kernel, ts=ts, tiles_per_seq=seq // ts)
    vec = _resident((1, CONV_DIM))
    return pl.pallas_call(
        kern,
        grid=(m // ts,),
        in_specs=[pl.BlockSpec((ts, 2 * CONV_DIM), lambda i: (i, 0)),
                  pl.BlockSpec((CONV_HALO, 2 * CONV_DIM), lambda i: (jnp.maximum(i * hb - 1, 0), 0)),
                  _resident((CONV_WIDTH, CONV_DIM)), vec, vec, vec],
        out_specs=pl.BlockSpec((ts, CONV_DIM), lambda i: (i, 0)),
        out_shape=jax.ShapeDtypeStruct((m, CONV_DIM), BF16),
        scratch_shapes=[pltpu.VMEM((ts + CONV_HALO, CONV_DIM), F32), pltpu.VMEM((ts, CONV_DIM), F32)],
        compiler_params=_params(("parallel",), 0),
        name="conv",
    )(z_conv, z_conv, conv_w, conv_b, norm_g, norm_b)


def _gelu_tanh(x):
    return 0.5 * x * (1.0 + jnp.tanh(0.7978845608028654 * (x + 0.044715 * (x * x * x))))


def _sgu_kernel(z_ref, ng_ref, nb_ref, w_ref, bt_ref, o_ref, *, ts):
    z = _gelu_tanh(z_ref[...].astype(F32))
    u = z[:, :SGU_DIM]
    v = _layernorm(z[:, SGU_DIM:], ng_ref[...], nb_ref[...]).astype(BF16)
    row = lax.broadcasted_iota(jnp.int32, (CHUNK, CHUNK), 0)
    col = lax.broadcasted_iota(jnp.int32, (CHUNK, CHUNK), 1)
    n_chunks = ts // CHUNK
    for g in range(SGU_GROUPS):
        c0, c1 = g * SGU_GDIM, (g + 1) * SGU_GDIM
        w = jnp.where(col <= row, w_ref[g], 0.0).astype(BF16)
        rhs = jnp.concatenate([v[n * CHUNK:(n + 1) * CHUNK, c0:c1] for n in range(n_chunks)], axis=1)
        sp = _dot(w, rhs) + bt_ref[:, g:g + 1]
        for n in range(n_chunks):
            o_ref[n * CHUNK:(n + 1) * CHUNK, c0:c1] = (
                u[n * CHUNK:(n + 1) * CHUNK, c0:c1] * sp[:, n * SGU_GDIM:(n + 1) * SGU_GDIM]).astype(BF16)


def _sgu(z_sgu, norm_g, norm_b, sgu_w, sgu_bt, ts):
    m = z_sgu.shape[0]
    vec = _resident((1, SGU_DIM))
    return pl.pallas_call(
        functools.partial(_sgu_kernel, ts=ts),
        grid=(m // ts,),
        in_specs=[pl.BlockSpec((ts, 2 * SGU_DIM), lambda i: (i, 0)), vec, vec,
                  _resident(sgu_w.shape), _resident(sgu_bt.shape)],
        out_specs=pl.BlockSpec((ts, SGU_DIM), lambda i: (i, 0)),
        out_shape=jax.ShapeDtypeStruct((m, SGU_DIM), BF16),
        compiler_params=_params(("parallel",), 0),
        name="sgu",
    )(z_sgu, norm_g, norm_b, sgu_w, sgu_bt)


def _mla_prep_kernel(cq_ref, ckv_ref, kr_ref, c_ref, s1_ref, s2_ref, qg_ref, kvg_ref, wq_ref, wkv_ref,
                     q_ref, kn_ref, kro_ref, v_ref):
    c, s1, s2 = c_ref[...], s1_ref[...], s2_ref[...]
    scale = (QK_NOPE + QK_ROPE) ** -0.5
    qn = _rms(cq_ref[...].astype(F32), qg_ref[...]).astype(BF16)
    for h in range(MLA_HEADS):
        o = h * Q_HEAD_PAD
        qh = _dot(qn, wq_ref[:, o:o + Q_HEAD_PAD]) * scale
        q_ref[:, o:o + LANES] = qh[:, :LANES].astype(BF16)
        q_ref[:, o + LANES:o + Q_HEAD_PAD] = _rope(qh[:, LANES:], c, s1, s2).astype(BF16)
    kvn = _rms(ckv_ref[...].astype(F32), kvg_ref[...]).astype(BF16)
    width = MLA_HEADS * QK_NOPE
    for o in range(0, width, 512):
        kn_ref[:, o:o + 512] = _dot(kvn, wkv_ref[:, o:o + 512]).astype(BF16)
        v_ref[:, o:o + 512] = _dot(kvn, wkv_ref[:, width + o:width + o + 512]).astype(BF16)
    kro_ref[...] = _rope(kr_ref[...].astype(F32), c, s1, s2).astype(BF16)


def _mla_prep(cq, ckv, kr, tables, q_norm_g, kv_norm_g, wq, wkv, ts):
    m = cq.shape[0]
    row = lambda i: (i, 0)
    tab = pl.BlockSpec((ts, LANES), row)
    hq = MLA_HEADS * Q_HEAD_PAD
    hk = MLA_HEADS * QK_NOPE
    return pl.pallas_call(
        _mla_prep_kernel,
        grid=(m // ts,),
        in_specs=[pl.BlockSpec((ts, Q_LORA), row), pl.BlockSpec((ts, KV_LORA), row),
                  pl.BlockSpec((ts, KR_PAD), row), tab, tab, tab,
                  _resident((1, Q_LORA)), _resident((1, KV_LORA)),
                  _resident(wq.shape), _resident(wkv.shape)],
        out_specs=[pl.BlockSpec((ts, hq), row), pl.BlockSpec((ts, hk), row),
                   pl.BlockSpec((ts, KR_PAD), row), pl.BlockSpec((ts, hk), row)],
        out_shape=[jax.ShapeDtypeStruct((m, hq), BF16), jax.ShapeDtypeStruct((m, hk), BF16),
                   jax.ShapeDtypeStruct((m, KR_PAD), BF16), jax.ShapeDtypeStruct((m, hk), BF16)],
        compiler_params=_params(("parallel",), 32 * 1024 * 1024),
        name="mla_prep",
    )(cq, ckv, kr, *tables, q_norm_g, kv_norm_g, wq, wkv)


def _attn_kernel(q_ref, kn_ref, kr_ref, v_ref, o_ref, *, tq):
    qi = pl.program_id(2)
    q = q_ref[...]

    def step(j, carry, masked):
        m, l, acc = carry
        k0 = pl.multiple_of(j * tq, tq)
        k = jnp.concatenate([kn_ref[pl.ds(k0, tq), :], kr_ref[pl.ds(k0, tq), :]], axis=-1)
        s = lax.dot_general(q, k, (((1,), (1,)), ((), ())), preferred_element_type=F32)
        if masked:
            row = lax.broadcasted_iota(jnp.int32, s.shape, 0)
            col = lax.broadcasted_iota(jnp.int32, s.shape, 1)
            s = jnp.where(col <= row, s, NEG_BIG)
        m_new = jnp.maximum(m, jnp.max(s, axis=-1, keepdims=True))
        alpha = jnp.exp(m - m_new)
        p = jnp.exp(s - m_new)
        l = alpha * l + jnp.sum(p, axis=-1, keepdims=True)
        acc = alpha * acc + _dot(p.astype(BF16), v_ref[pl.ds(k0, tq), :])
        return m_new, l, acc

    init = (jnp.full((tq, 1), NEG_BIG, F32), jnp.zeros((tq, 1), F32), jnp.zeros((tq, V_DIM), F32))
    carry = lax.fori_loop(0, qi, lambda j, c: step(j, c, False), init)
    _, l, acc = step(qi, carry, True)
    o_ref[...] = (acc / l).astype(BF16)


def _attention(q, kn, kr, v, batch, seq, tq):
    m = q.shape[0]
    nq = seq // tq
    kern = functools.partial(_attn_kernel, tq=tq)
    return pl.pallas_call(
        kern,
        grid=(batch, MLA_HEADS, nq),
        in_specs=[pl.BlockSpec((tq, Q_HEAD_PAD), lambda b, h, i: (b * nq + i, h)),
                  pl.BlockSpec((seq, QK_NOPE), lambda b, h, i: (b, h)),
                  pl.BlockSpec((seq, KR_PAD), lambda b, h, i: (b, 0)),
                  pl.BlockSpec((seq, V_DIM), lambda b, h, i: (b, h))],
        out_specs=pl.BlockSpec((tq, V_DIM), lambda b, h, i: (b * nq + i, h)),
        out_shape=jax.ShapeDtypeStruct((m, MLA_HEADS * V_DIM), BF16),
        compiler_params=_params(("parallel", "parallel", "arbitrary"), 32 * 1024 * 1024),
        name="attention",
    )(q, kn, kr, v)


def _merge_kernel(h_ref, bp_ref, bc_ref, bs_ref, ba_ref, g0_ref, g1_ref, g2_ref, g3_ref,
                  pp_ref, pc_ref, ps_ref, pa_ref, o_ref):
    h = h_ref[...]
    acc = None
    for b_ref, g_ref, p_ref in ((bp_ref, g0_ref, pp_ref), (bc_ref, g1_ref, pc_ref),
                                (bs_ref, g2_ref, ps_ref), (ba_ref, g3_ref, pa_ref)):
        term = jax.nn.sigmoid(_dot(h, g_ref[...])) * _dot(b_ref[...], p_ref[...])
        acc = term if acc is None else acc + term
    o_ref[...] = acc.astype(BF16)


def _merge(h, branches, w_gate, projs, tm, tn):
    m = h.shape[0]
    nj = D_MODEL // tn
    row = lambda i, j: (i, 0)
    in_specs = [pl.BlockSpec((tm, D_MODEL), row)]
    in_specs += [pl.BlockSpec((tm, b.shape[1]), row) for b in branches]
    in_specs += [pl.BlockSpec((D_MODEL, tn), functools.partial(lambda i, j, b: (0, b * nj + j), b=b))
                 for b in range(N_BRANCH)]
    in_specs += [pl.BlockSpec((p.shape[0], tn), lambda i, j: (0, j)) for p in projs]
    kin = sum(b.shape[1] for b in branches)
    vmem = 2 * 2 * (tm * D_MODEL + tm * kin + N_BRANCH * D_MODEL * tn + kin * tn + tm * tn) + 6 * tm * tn * 4
    return pl.pallas_call(
        _merge_kernel,
        grid=(m // tm, nj),
        in_specs=in_specs,
        out_specs=pl.BlockSpec((tm, tn), lambda i, j: (i, j)),
        out_shape=jax.ShapeDtypeStruct((m, D_MODEL), BF16),
        compiler_params=_params(("parallel", "arbitrary"), vmem),
        name="merge",
    )(h, *branches, w_gate, w_gate, w_gate, w_gate, *projs)


def _outproj_kernel(m_ref, w_ref, x_ref, gpost_ref, gpre_ref, x1_ref, h2_ref, y_ref):
    mg = m_ref[...]
    for c in range(0, D_MODEL, 512):
        y_ref[:, c:c + 512] = _dot(mg, w_ref[:, c:c + 512])
    x1 = x_ref[...] + _rms(y_ref[...], gpost_ref[...])
    x1_ref[...] = x1
    h2_ref[...] = _rms(x1, gpre_ref[...]).astype(BF16)


def _outproj(merged, w_out, x, g_post, g_pre, tm):
    m = x.shape[0]
    row = lambda i: (i, 0)
    vec = _resident((1, D_MODEL))
    vmem = 2 * tm * D_MODEL * (2 + 4 + 4 + 2) + D_MODEL * D_MODEL * 2 + 4 * tm * D_MODEL * 4
    return pl.pallas_call(
        _outproj_kernel,
        grid=(m // tm,),
        in_specs=[pl.BlockSpec((tm, D_MODEL), row), _resident((D_MODEL, D_MODEL)),
                  pl.BlockSpec((tm, D_MODEL), row), vec, vec],
        out_specs=[pl.BlockSpec((tm, D_MODEL), row), pl.BlockSpec((tm, D_MODEL), row)],
        out_shape=[jax.ShapeDtypeStruct((m, D_MODEL), F32), jax.ShapeDtypeStruct((m, D_MODEL), BF16)],
        scratch_shapes=[pltpu.VMEM((tm, D_MODEL), F32)],
        compiler_params=_params(("parallel",), vmem),
        name="outproj",
    )(merged, w_out, x, g_post, g_pre)


def _mlp_kernel(h_ref, wu_ref, wd_ref, x_ref, g_ref, o_ref, acc_ref):
    f = pl.program_id(1)
    a = jnp.square(jnp.maximum(_dot(h_ref[...], wu_ref[...]), 0.0)).astype(BF16)
    d = _dot(a, wd_ref[...])

    @pl.when(f == 0)
    def _():
        acc_ref[...] = d

    @pl.when(f > 0)
    def _():
        acc_ref[...] += d

    @pl.when(f == pl.num_programs(1) - 1)
    def _():
        o_ref[...] = x_ref[...] + _rms(acc_ref[...], g_ref[...])


def _mlp(h2, w_up, w_down, x1, g_post, tm, tf):
    m = x1.shape[0]
    row = lambda i, f: (i, 0)
    vmem = (2 * tm * D_MODEL * (2 + 4 + 4) + 2 * 2 * 2 * D_MODEL * tf + tm * D_MODEL * 4
            + 2 * tm * D_MODEL * 4 + 2 * tm * tf * 4)
    return pl.pallas_call(
        _mlp_kernel,
        grid=(m // tm, D_FF // tf),
        in_specs=[pl.BlockSpec((tm, D_MODEL), row), pl.BlockSpec((D_MODEL, tf), lambda i, f: (0, f)),
                  pl.BlockSpec((tf, D_MODEL), lambda i, f: (f, 0)), pl.BlockSpec((tm, D_MODEL), row),
                  pl.BlockSpec((1, D_MODEL), lambda i, f: (0, 0))],
        out_specs=pl.BlockSpec((tm, D_MODEL), row),
        out_shape=jax.ShapeDtypeStruct((m, D_MODEL), F32),
        scratch_shapes=[pltpu.VMEM((tm, D_MODEL), F32)],
        compiler_params=_params(("parallel", "arbitrary"), vmem),
        name="mlp",
    )(h2, w_up, w_down, x1, g_post)


def _small_in_weight(w_in_l):
    pad = jnp.zeros((D_MODEL, KR_PAD - QK_ROPE), w_in_l.dtype)
    return jnp.concatenate([w_in_l[:, :OFF_GATE], pad], axis=1).astype(BF16)


def _q_weight(w_uq_l):
    w = w_uq_l.reshape(Q_LORA, MLA_HEADS, QK_NOPE + QK_ROPE)
    pad = jnp.zeros((Q_LORA, MLA_HEADS, Q_HEAD_PAD - QK_NOPE - QK_ROPE), w.dtype)
    return jnp.concatenate([w, pad], axis=2).reshape(Q_LORA, MLA_HEADS * Q_HEAD_PAD).astype(BF16)


def _kv_weight(w_ukv_l):
    w = w_ukv_l.reshape(KV_LORA, MLA_HEADS, QK_NOPE + V_DIM)
    k = w[:, :, :QK_NOPE].reshape(KV_LORA, MLA_HEADS * QK_NOPE)
    v = w[:, :, QK_NOPE:].reshape(KV_LORA, MLA_HEADS * V_DIM)
    return jnp.concatenate([k, v], axis=1).astype(BF16)


def _tile(n, want):
    t = min(n, want)
    assert n % t == 0, (n, t)
    return t


def kernel(x, positions, pre_mix_g, w_in, pool_w, pool_scale, pool_proj, conv_w, conv_b, conv_norm_g, conv_norm_b, conv_proj, sgu_norm_g, sgu_norm_b, sgu_w, sgu_b, sgu_proj, q_norm_g, w_uq, kv_norm_g, w_ukv, attn_proj, w_out, post_mix_g, pre_mlp_g, w_up, w_down, post_mlp_g):
    batch, seq, d = x.shape
    assert d == D_MODEL and seq % CHUNK == 0
    m = batch * seq
    depth = w_in.shape[0]
    ts = _tile(seq, 512)
    ts_conv = _tile(seq, 256)
    tq = _tile(seq, 512)
    tm = _tile(m, 512)
    tm_mlp = _tile(m, 512)

    xf = x.reshape(m, D_MODEL)
    tables = _rope_tables(positions.reshape(m, 1), ts)
    row = lambda v: v[None, :]

    for l in range(depth):
        h, z_pool, z_conv, z_sgu, cq, ckv, kr = _inproj(xf, row(pre_mix_g[l]), _small_in_weight(w_in[l]), tm)
        y_pool = _pool(z_pool, pool_w[l].astype(BF16), row(pool_scale[l]), ts, seq)
        y_conv = _conv(z_conv, conv_w[l], row(conv_b[l]), row(conv_norm_g[l]), row(conv_norm_b[l]), ts_conv, seq)
        y_sgu = _sgu(z_sgu, row(sgu_norm_g[l]), row(sgu_norm_b[l]), sgu_w[l], sgu_b[l].T, ts)
        q, kn, kro, v = _mla_prep(cq, ckv, kr, tables, row(q_norm_g[l]), row(kv_norm_g[l]),
                                  _q_weight(w_uq[l]), _kv_weight(w_ukv[l]), ts)
        y_attn = _attention(q, kn, kro, v, batch, seq, tq)
        merged = _merge(h, (y_pool, y_conv, y_sgu, y_attn), w_in[l][:, OFF_GATE:].astype(BF16),
                        (pool_proj[l].astype(BF16), conv_proj[l].astype(BF16), sgu_proj[l].astype(BF16),
                         attn_proj[l].astype(BF16)), tm, 512)
        x1, h2 = _outproj(merged, w_out[l].astype(BF16), xf, row(post_mix_g[l]), row(pre_mlp_g[l]), tm)
        xf = _mlp(h2, w_up[l].astype(BF16), w_down[l].astype(BF16), x1, row(post_mlp_g[l]), tm_mlp, 512)
    return xf.reshape(batch, seq, D_MODEL)
```

```python
import functools

import jax
import jax.numpy as jnp
from jax import lax
from jax.experimental import pallas as pl
from jax.experimental.pallas import tpu as pltpu

F32 = jnp.float32
BF16 = jnp.bfloat16

D_MODEL = 2048
EPS = 1e-6
N_BRANCH = 4
D_FF = 4 * D_MODEL
POOL_DIM = 512
POOL_WINDOWS = (2, 4, 8, 16)
POOL_GDIM = POOL_DIM // len(POOL_WINDOWS)
CONV_DIM = 512
CONV_WIDTH = 31
SGU_DIM = 512
SGU_GROUPS = 4
SGU_GDIM = SGU_DIM // SGU_GROUPS
CHUNK = 128
MLA_HEADS = 8
Q_LORA = 512
KV_LORA = 512
QK_NOPE = 128
QK_ROPE = 64
V_DIM = 128
ROPE_THETA = 10000.0
OFF_KR = POOL_DIM + 2 * CONV_DIM + 2 * SGU_DIM + Q_LORA + KV_LORA
OFF_GATE = OFF_KR + QK_ROPE

LANES = 128
Q_HEAD_PAD = 2 * LANES
KR_PAD = LANES
N_SMALL = OFF_KR + KR_PAD
POOL_HALO = 16
CONV_HALO = 32
NEG_BIG = -1e30
LOG2_E = 1.4426950408889634
VMEM_CAP = 56 * 1024 * 1024

_SEGS = (POOL_DIM, 2 * CONV_DIM, 2 * SGU_DIM, Q_LORA, KV_LORA, KR_PAD)


def _params(semantics, vmem_bytes):
    return pltpu.CompilerParams(dimension_semantics=semantics,
                                vmem_limit_bytes=int(min(VMEM_CAP, max(vmem_bytes, 16 * 1024 * 1024))))


def _resident(shape):
    nd = len(shape)
    return pl.BlockSpec(shape, lambda *_: (0,) * nd, pipeline_mode=pl.Buffered(1))


def _rms(x, g):
    return x * lax.rsqrt(jnp.mean(x * x, axis=-1, keepdims=True) + EPS) * g


def _layernorm(x, g, b):
    mu = jnp.mean(x, axis=-1, keepdims=True)
    xc = x - mu
    var = jnp.mean(xc * xc, axis=-1, keepdims=True)
    return xc * lax.rsqrt(var + EPS) * g + b


def _dot(a, b):
    return jnp.dot(a, b, preferred_element_type=F32)


def _rope_table_kernel(pos_ref, freq_ref, c_ref, s1_ref, s2_ref):
    ang = pos_ref[...].astype(F32) * freq_ref[...]
    lane = lax.broadcasted_iota(jnp.int32, ang.shape, 1)
    cos = jnp.cos(ang)
    sin = jnp.sin(ang)
    half = QK_ROPE // 2
    c_ref[...] = jnp.where(lane < QK_ROPE, cos, 0.0)
    s1_ref[...] = jnp.where(lane < half, 0.0, jnp.where(lane < QK_ROPE, sin, 0.0))
    s2_ref[...] = jnp.where(lane < half, -sin, 0.0)


def _rope_tables(pos, tm):
    m = pos.shape[0]
    inv_freq = ROPE_THETA ** (-jnp.arange(0, QK_ROPE, 2, dtype=F32) / QK_ROPE)
    freq = jnp.concatenate([inv_freq, inv_freq, jnp.zeros((LANES - QK_ROPE,), F32)])[None, :]
    out = jax.ShapeDtypeStruct((m, LANES), F32)
    spec = pl.BlockSpec((tm, LANES), lambda i: (i, 0))
    return pl.pallas_call(
        _rope_table_kernel,
        grid=(m // tm,),
        in_specs=[pl.BlockSpec((tm, 1), lambda i: (i, 0)), pl.BlockSpec((1, LANES), lambda i: (0, 0))],
        out_specs=[spec, spec, spec],
        out_shape=[out, out, out],
        compiler_params=_params(("parallel",), 0),
        name="rope_tables",
    )(pos, freq)


def _rope(chunk, c, s1, s2):
    half = QK_ROPE // 2
    return (chunk * c + pltpu.roll(chunk, half, 1) * s1
            + pltpu.roll(chunk, LANES - half, 1) * s2)


def _inproj_kernel(x_ref, g_ref, w_ref, h_ref, *z_refs):
    h = _rms(x_ref[...], g_ref[...]).astype(BF16)
    h_ref[...] = h
    off = 0
    for z_ref, width in zip(z_refs, _SEGS):
        for c in range(0, width, 512):
            cw = min(512, width - c)
            z_ref[:, c:c + cw] = _dot(h, w_ref[:, off + c:off + c + cw]).astype(BF16)
        off += width


def _inproj(x, g, w_small, tm):
    m = x.shape[0]
    row = lambda i: (i, 0)
    out_shape = [jax.ShapeDtypeStruct((m, D_MODEL), BF16)]
    out_specs = [pl.BlockSpec((tm, D_MODEL), row)]
    for width in _SEGS:
        out_shape.append(jax.ShapeDtypeStruct((m, width), BF16))
        out_specs.append(pl.BlockSpec((tm, width), row))
    vmem = (2 * tm * D_MODEL * 4 + 2 * tm * D_MODEL * 2 + D_MODEL * N_SMALL * 2
            + 2 * tm * N_SMALL * 2 + 4 * tm * D_MODEL * 4)
    return pl.pallas_call(
        _inproj_kernel,
        grid=(m // tm,),
        in_specs=[pl.BlockSpec((tm, D_MODEL), row), _resident((1, D_MODEL)),
                  _resident((D_MODEL, N_SMALL))],
        out_specs=out_specs,
        out_shape=out_shape,
        compiler_params=_params(("parallel",), vmem),
        name="inproj",
    )(x, g, w_small)


def _pool_kernel(a_ref, halo_ref, w_ref, scale_ref, o_ref, ext_ref, *, ts, tiles_per_seq):
    i = pl.program_id(0)
    first = (i % tiles_per_seq) == 0
    a = a_ref[...].astype(F32)
    ext_ref[0:POOL_HALO, :] = jnp.where(first, 0.0, halo_ref[...].astype(F32))
    ext_ref[POOL_HALO:, :] = a
    t = (i % tiles_per_seq) * ts + lax.broadcasted_iota(jnp.int32, (ts, 1), 0)
    for gi, w in enumerate(POOL_WINDOWS):
        c0, c1 = gi * POOL_GDIM, (gi + 1) * POOL_GDIM
        win = ext_ref[POOL_HALO:POOL_HALO + ts, c0:c1]
        for d in range(1, w):
            win = win + ext_ref[POOL_HALO - d:POOL_HALO - d + ts, c0:c1]
        count = jnp.minimum(t + 1, w).astype(F32)
        pooled = (win / count - a[:, c0:c1]).astype(BF16)
        o_ref[:, c0:c1] = (_dot(pooled, w_ref[gi]) * scale_ref[:, c0:c1]).astype(BF16)


def _pool(z_pool, pool_w, pool_scale, ts, seq):
    m = z_pool.shape[0]
    hb = ts // POOL_HALO
    kern = functools.partial(_pool_kernel, ts=ts, tiles_per_seq=seq // ts)
    return pl.pallas_call(
        kern,
        grid=(m // ts,),
        in_specs=[pl.BlockSpec((ts, POOL_DIM), lambda i: (i, 0)),
                  pl.BlockSpec((POOL_HALO, POOL_DIM), lambda i: (jnp.maximum(i * hb - 1, 0), 0)),
                  _resident(pool_w.shape), _resident((1, POOL_DIM))],
        out_specs=pl.BlockSpec((ts, POOL_DIM), lambda i: (i, 0)),
        out_shape=jax.ShapeDtypeStruct((m, POOL_DIM), BF16),
        scratch_shapes=[pltpu.VMEM((ts + POOL_HALO, POOL_DIM), F32)],
        compiler_params=_params(("parallel",), 0),
        name="pool",
    )(z_pool, z_pool, pool_w, pool_scale)


_CONV_ROWS = 64


def _glu(c):
    c = c.astype(F32)
    return c[:, :CONV_DIM] * jax.nn.sigmoid(c[:, CONV_DIM:])


def _conv_kernel(c_ref, halo_ref, w_ref, b_ref, ng_ref, nb_ref, o_ref, ext_ref, y_ref, *, ts, tiles_per_seq):
    i = pl.program_id(0)
    first = (i % tiles_per_seq) == 0
    ext_ref[0:CONV_HALO, :] = jnp.where(first, 0.0, _glu(halo_ref[...]))
    ext_ref[CONV_HALO:, :] = _glu(c_ref[...])
    base = CONV_HALO - (CONV_WIDTH - 1)
    for r in range(0, ts, _CONV_ROWS):
        for c in range(0, CONV_DIM, LANES):
            acc = None
            for k in range(CONV_WIDTH):
                term = ext_ref[base + r + k:base + r + k + _CONV_ROWS, c:c + LANES] * w_ref[k:k + 1, c:c + LANES]
                acc = term if acc is None else acc + term
            y_ref[r:r + _CONV_ROWS, c:c + LANES] = acc
    y = _layernorm(y_ref[...] + b_ref[...], ng_ref[...], nb_ref[...])
    o_ref[...] = (y * jax.nn.sigmoid(y)).astype(BF16)


def _conv(z_conv, conv_w, conv_b, norm_g, norm_b, ts, seq):
    m = z_conv.shape[0]
    hb = ts // CONV_HALO
    kern = functools.partial(_conv_kernel, ts=ts, tiles_per_seq=seq // ts)
    vec = _resident((1, CONV_DIM))
    return pl.pallas_call(
        kern,
        grid=(m // ts,),
        in_specs=[pl.BlockSpec((ts, 2 * CONV_DIM), lambda i: (i, 0)),
                  pl.BlockSpec((CONV_HALO, 2 * CONV_DIM), lambda i: (jnp.maximum(i * hb - 1, 0), 0)),
                  _resident((CONV_WIDTH, CONV_DIM)), vec, vec, vec],
        out_specs=pl.BlockSpec((ts, CONV_DIM), lambda i: (i, 0)),
        out_shape=jax.ShapeDtypeStruct((m, CONV_DIM), BF16),
        scratch_shapes=[pltpu.VMEM((ts + CONV_HALO, CONV_DIM), F32), pltpu.VMEM((ts, CONV_DIM), F32)],
        compiler_params=_params(("parallel",), 0),
        name="conv",
    )(z_conv, z_conv, conv_w, conv_b, norm_g, norm_b)


def _gelu_tanh(x):
    return 0.5 * x * (1.0 + jnp.tanh(0.7978845608028654 * (x + 0.044715 * (x * x * x))))


def _sgu_kernel(z_ref, ng_ref, nb_ref, w_ref, bt_ref, o_ref, *, ts):
    z = _gelu_tanh(z_ref[...].astype(F32))
    u = z[:, :SGU_DIM]
    v = _layernorm(z[:, SGU_DIM:], ng_ref[...], nb_ref[...]).astype(BF16)
    row = lax.broadcasted_iota(jnp.int32, (CHUNK, CHUNK), 0)
    col = lax.broadcasted_iota(jnp.int32, (CHUNK, CHUNK), 1)
    n_chunks = ts // CHUNK
    for g in range(SGU_GROUPS):
        c0, c1 = g * SGU_GDIM, (g + 1) * SGU_GDIM
        w = jnp.where(col <= row, w_ref[g], 0.0).astype(BF16)
        rhs = jnp.concatenate([v[n * CHUNK:(n + 1) * CHUNK, c0:c1] for n in range(n_chunks)], axis=1)
        sp = _dot(w, rhs) + bt_ref[:, g:g + 1]
        for n in range(n_chunks):
            o_ref[n * CHUNK:(n + 1) * CHUNK, c0:c1] = (
                u[n * CHUNK:(n + 1) * CHUNK, c0:c1] * sp[:, n * SGU_GDIM:(n + 1) * SGU_GDIM]).astype(BF16)


def _sgu(z_sgu, norm_g, norm_b, sgu_w, sgu_bt, ts):
    m = z_sgu.shape[0]
    vec = _resident((1, SGU_DIM))
    return pl.pallas_call(
        functools.partial(_sgu_kernel, ts=ts),
        grid=(m // ts,),
        in_specs=[pl.BlockSpec((ts, 2 * SGU_DIM), lambda i: (i, 0)), vec, vec,
                  _resident(sgu_w.shape), _resident(sgu_bt.shape)],
        out_specs=pl.BlockSpec((ts, SGU_DIM), lambda i: (i, 0)),
        out_shape=jax.ShapeDtypeStruct((m, SGU_DIM), BF16),
        compiler_params=_params(("parallel",), 0),
        name="sgu",
    )(z_sgu, norm_g, norm_b, sgu_w, sgu_bt)


def _mla_prep_kernel(cq_ref, ckv_ref, kr_ref, c_ref, s1_ref, s2_ref, qg_ref, kvg_ref, wq_ref, wkv_ref,
                     q_ref, kn_ref, kro_ref, v_ref):
    c, s1, s2 = c_ref[...], s1_ref[...], s2_ref[...]
    scale = (QK_NOPE + QK_ROPE) ** -0.5 * LOG2_E
    qn = _rms(cq_ref[...].astype(F32), qg_ref[...]).astype(BF16)
    for h in range(MLA_HEADS):
        o = h * Q_HEAD_PAD
        qh = _dot(qn, wq_ref[:, o:o + Q_HEAD_PAD]) * scale
        q_ref[:, o:o + LANES] = qh[:, :LANES].astype(BF16)
        q_ref[:, o + LANES:o + Q_HEAD_PAD] = _rope(qh[:, LANES:], c, s1, s2).astype(BF16)
    kvn = _rms(ckv_ref[...].astype(F32), kvg_ref[...]).astype(BF16)
    width = MLA_HEADS * QK_NOPE
    for o in range(0, width, 512):
        kn_ref[:, o:o + 512] = _dot(kvn, wkv_ref[:, o:o + 512]).astype(BF16)
        v_ref[:, o:o + 512] = _dot(kvn, wkv_ref[:, width + o:width + o + 512]).astype(BF16)
    kro_ref[...] = _rope(kr_ref[...].astype(F32), c, s1, s2).astype(BF16)


def _mla_prep(cq, ckv, kr, tables, q_norm_g, kv_norm_g, wq, wkv, ts):
    m = cq.shape[0]
    row = lambda i: (i, 0)
    tab = pl.BlockSpec((ts, LANES), row)
    hq = MLA_HEADS * Q_HEAD_PAD
    hk = MLA_HEADS * QK_NOPE
    return pl.pallas_call(
        _mla_prep_kernel,
        grid=(m // ts,),
        in_specs=[pl.BlockSpec((ts, Q_LORA), row), pl.BlockSpec((ts, KV_LORA), row),
                  pl.BlockSpec((ts, KR_PAD), row), tab, tab, tab,
                  _resident((1, Q_LORA)), _resident((1, KV_LORA)),
                  _resident(wq.shape), _resident(wkv.shape)],
        out_specs=[pl.BlockSpec((ts, hq), row), pl.BlockSpec((ts, hk), row),
                   pl.BlockSpec((ts, KR_PAD), row), pl.BlockSpec((ts, hk), row)],
        out_shape=[jax.ShapeDtypeStruct((m, hq), BF16), jax.ShapeDtypeStruct((m, hk), BF16),
                   jax.ShapeDtypeStruct((m, KR_PAD), BF16), jax.ShapeDtypeStruct((m, hk), BF16)],
        compiler_params=_params(("parallel",), 32 * 1024 * 1024),
        name="mla_prep",
    )(cq, ckv, kr, *tables, q_norm_g, kv_norm_g, wq, wkv)


_ATTN_ROWS = 64


def _attn_kernel(q_ref, kn_ref, kr_ref, v_ref, o_ref, s_ref, p_ref, m_ref, l_ref, acc_ref, *, tq, tk):
    qi = pl.program_id(2)
    n_diag = tq // tk
    m_ref[...] = jnp.full(m_ref.shape, NEG_BIG, F32)
    l_ref[...] = jnp.zeros(l_ref.shape, F32)
    acc_ref[...] = jnp.zeros(acc_ref.shape, F32)

    def block(j, row_start, mask_shift):
        k0 = pl.multiple_of(j * tk, tk)
        k = jnp.concatenate([kn_ref[pl.ds(k0, tk), :], kr_ref[pl.ds(k0, tk), :]], axis=-1)
        s_ref[row_start:, :] = lax.dot_general(q_ref[row_start:, :], k, (((1,), (1,)), ((), ())),
                                               preferred_element_type=F32)
        for c in range(row_start, tq, _ATTN_ROWS):
            rows = slice(c, c + _ATTN_ROWS)
            s = s_ref[rows, :]
            if mask_shift is not None and mask_shift + tk - 1 > c:
                row = lax.broadcasted_iota(jnp.int32, s.shape, 0) + c
                col = lax.broadcasted_iota(jnp.int32, s.shape, 1) + mask_shift
                s = jnp.where(col <= row, s, NEG_BIG)
            m_prev = m_ref[rows, :]
            m_next = jnp.maximum(m_prev, jnp.max(s, axis=1, keepdims=True))
            alpha = jnp.exp2(m_prev - m_next)
            p = jnp.exp2(s - pltpu.repeat(m_next, tk // LANES, axis=1))
            l_ref[rows, :] = alpha * l_ref[rows, :] + jnp.sum(p, axis=1, keepdims=True)
            m_ref[rows, :] = m_next
            acc_ref[rows, :] = alpha * acc_ref[rows, :]
            p_ref[rows, :] = p.astype(BF16)
        acc_ref[row_start:, :] += _dot(p_ref[row_start:, :], v_ref[pl.ds(k0, tk), :])

    def body(j, carry):
        block(j, 0, None)
        return carry

    lax.fori_loop(0, qi * n_diag, body, 0)
    for d in range(n_diag):
        block(qi * n_diag + d, d * tk, d * tk)
    o_ref[...] = (acc_ref[...] / l_ref[...]).astype(BF16)


def _attention(q, kn, kr, v, batch, seq, tq, tk):
    m = q.shape[0]
    nq = seq // tq
    kern = functools.partial(_attn_kernel, tq=tq, tk=tk)
    return pl.pallas_call(
        kern,
        grid=(batch, MLA_HEADS, nq),
        in_specs=[pl.BlockSpec((tq, Q_HEAD_PAD), lambda b, h, i: (b * nq + i, h)),
                  pl.BlockSpec((seq, QK_NOPE), lambda b, h, i: (b, h)),
                  pl.BlockSpec((seq, KR_PAD), lambda b, h, i: (b, 0)),
                  pl.BlockSpec((seq, V_DIM), lambda b, h, i: (b, h))],
        out_specs=pl.BlockSpec((tq, V_DIM), lambda b, h, i: (b * nq + i, h)),
        out_shape=jax.ShapeDtypeStruct((m, MLA_HEADS * V_DIM), BF16),
        scratch_shapes=[pltpu.VMEM((tq, tk), F32), pltpu.VMEM((tq, tk), BF16), pltpu.VMEM((tq, LANES), F32),
                        pltpu.VMEM((tq, LANES), F32), pltpu.VMEM((tq, V_DIM), F32)],
        compiler_params=_params(("parallel", "parallel", "arbitrary"), 32 * 1024 * 1024),
        name="attention",
    )(q, kn, kr, v)


def _merge_kernel(h_ref, bp_ref, bc_ref, bs_ref, ba_ref, g0_ref, g1_ref, g2_ref, g3_ref,
                  pp_ref, pc_ref, ps_ref, pa_ref, o_ref):
    h = h_ref[...]
    acc = None
    for b_ref, g_ref, p_ref in ((bp_ref, g0_ref, pp_ref), (bc_ref, g1_ref, pc_ref),
                                (bs_ref, g2_ref, ps_ref), (ba_ref, g3_ref, pa_ref)):
        term = jax.nn.sigmoid(_dot(h, g_ref[...])) * _dot(b_ref[...], p_ref[...])
        acc = term if acc is None else acc + term
    o_ref[...] = acc.astype(BF16)


def _merge(h, branches, w_gate, projs, tm, tn):
    m = h.shape[0]
    nj = D_MODEL // tn
    row = lambda i, j: (i, 0)
    in_specs = [pl.BlockSpec((tm, D_MODEL), row)]
    in_specs += [pl.BlockSpec((tm, b.shape[1]), row) for b in branches]
    in_specs += [pl.BlockSpec((D_MODEL, tn), functools.partial(lambda i, j, b: (0, b * nj + j), b=b))
                 for b in range(N_BRANCH)]
    in_specs += [pl.BlockSpec((p.shape[0], tn), lambda i, j: (0, j)) for p in projs]
    kin = sum(b.shape[1] for b in branches)
    vmem = 2 * 2 * (tm * D_MODEL + tm * kin + N_BRANCH * D_MODEL * tn + kin * tn + tm * tn) + 6 * tm * tn * 4
    return pl.pallas_call(
        _merge_kernel,
        grid=(m // tm, nj),
        in_specs=in_specs,
        out_specs=pl.BlockSpec((tm, tn), lambda i, j: (i, j)),
        out_shape=jax.ShapeDtypeStruct((m, D_MODEL), BF16),
        compiler_params=_params(("parallel", "arbitrary"), vmem),
        name="merge",
    )(h, *branches, w_gate, w_gate, w_gate, w_gate, *projs)


def _outproj_kernel(m_ref, w_ref, x_ref, gpost_ref, gpre_ref, x1_ref, h2_ref, y_ref):
    mg = m_ref[...]
    for c in range(0, D_MODEL, 512):
        y_ref[:, c:c + 512] = _dot(mg, w_ref[:, c:c + 512])
    x1 = x_ref[...] + _rms(y_ref[...], gpost_ref[...])
    x1_ref[...] = x1
    h2_ref[...] = _rms(x1, gpre_ref[...]).astype(BF16)


def _outproj(merged, w_out, x, g_post, g_pre, tm):
    m = x.shape[0]
    row = lambda i: (i, 0)
    vec = _resident((1, D_MODEL))
    vmem = 2 * tm * D_MODEL * (2 + 4 + 4 + 2) + D_MODEL * D_MODEL * 2 + 4 * tm * D_MODEL * 4
    return pl.pallas_call(
        _outproj_kernel,
        grid=(m // tm,),
        in_specs=[pl.BlockSpec((tm, D_MODEL), row), _resident((D_MODEL, D_MODEL)),
                  pl.BlockSpec((tm, D_MODEL), row), vec, vec],
        out_specs=[pl.BlockSpec((tm, D_MODEL), row), pl.BlockSpec((tm, D_MODEL), row)],
        out_shape=[jax.ShapeDtypeStruct((m, D_MODEL), F32), jax.ShapeDtypeStruct((m, D_MODEL), BF16)],
        scratch_shapes=[pltpu.VMEM((tm, D_MODEL), F32)],
        compiler_params=_params(("parallel",), vmem),
        name="outproj",
    )(merged, w_out, x, g_post, g_pre)


_MLP_COLS = 512


def _mlp_kernel(h_ref, wu_ref, wd_ref, x_ref, g_ref, o_ref, acc_ref, a_ref):
    f = pl.program_id(1)
    tf = a_ref.shape[1]

    @pl.when(f == 0)
    def _():
        acc_ref[...] = jnp.zeros_like(acc_ref)

    h = h_ref[...]
    for c in range(0, tf, _MLP_COLS):
        up = _dot(h, wu_ref[:, c:c + _MLP_COLS])
        a_ref[:, c:c + _MLP_COLS] = jnp.square(jnp.maximum(up, 0.0)).astype(BF16)
    a = a_ref[...]
    for c in range(0, D_MODEL, _MLP_COLS):
        acc_ref[:, c:c + _MLP_COLS] += _dot(a, wd_ref[:, c:c + _MLP_COLS])

    @pl.when(f == pl.num_programs(1) - 1)
    def _():
        o_ref[...] = x_ref[...] + _rms(acc_ref[...], g_ref[...])


def _mlp(h2, w_up, w_down, x1, g_post, tm, tf):
    m = x1.shape[0]
    row = lambda i, f: (i, 0)
    vmem = (tm * D_MODEL * (2 * 2 + 4 + 2 * 4 + 4) + 2 * 2 * 2 * D_MODEL * tf + tm * tf * 2
            + 3 * tm * _MLP_COLS * 4 + 4 * 1024 * 1024)
    return pl.pallas_call(
        _mlp_kernel,
        grid=(m // tm, D_FF // tf),
        in_specs=[pl.BlockSpec((tm, D_MODEL), row), pl.BlockSpec((D_MODEL, tf), lambda i, f: (0, f)),
                  pl.BlockSpec((tf, D_MODEL), lambda i, f: (f, 0)),
                  pl.BlockSpec((tm, D_MODEL), row, pipeline_mode=pl.Buffered(1)),
                  _resident((1, D_MODEL))],
        out_specs=pl.BlockSpec((tm, D_MODEL), row),
        out_shape=jax.ShapeDtypeStruct((m, D_MODEL), F32),
        scratch_shapes=[pltpu.VMEM((tm, D_MODEL), F32), pltpu.VMEM((tm, tf), BF16)],
        compiler_params=_params(("parallel", "arbitrary"), vmem),
        name="mlp",
    )(h2, w_up, w_down, x1, g_post)


def _small_in_weight(w_in_l):
    pad = jnp.zeros((D_MODEL, KR_PAD - QK_ROPE), w_in_l.dtype)
    return jnp.concatenate([w_in_l[:, :OFF_GATE], pad], axis=1).astype(BF16)


def _q_weight(w_uq_l):
    w = w_uq_l.reshape(Q_LORA, MLA_HEADS, QK_NOPE + QK_ROPE)
    pad = jnp.zeros((Q_LORA, MLA_HEADS, Q_HEAD_PAD - QK_NOPE - QK_ROPE), w.dtype)
    return jnp.concatenate([w, pad], axis=2).reshape(Q_LORA, MLA_HEADS * Q_HEAD_PAD).astype(BF16)


def _kv_weight(w_ukv_l):
    w = w_ukv_l.reshape(KV_LORA, MLA_HEADS, QK_NOPE + V_DIM)
    k = w[:, :, :QK_NOPE].reshape(KV_LORA, MLA_HEADS * QK_NOPE)
    v = w[:, :, QK_NOPE:].reshape(KV_LORA, MLA_HEADS * V_DIM)
    return jnp.concatenate([k, v], axis=1).astype(BF16)


def _tile(n, want):
    t = min(n, want)
    assert n % t == 0, (n, t)
    return t


def kernel(x, positions, pre_mix_g, w_in, pool_w, pool_scale, pool_proj, conv_w, conv_b, conv_norm_g, conv_norm_b, conv_proj, sgu_norm_g, sgu_norm_b, sgu_w, sgu_b, sgu_proj, q_norm_g, w_uq, kv_norm_g, w_ukv, attn_proj, w_out, post_mix_g, pre_mlp_g, w_up, w_down, post_mlp_g):
    batch, seq, d = x.shape
    assert d == D_MODEL and seq % CHUNK == 0
    m = batch * seq
    depth = w_in.shape[0]
    ts = _tile(seq, 512)
    ts_conv = _tile(seq, 256)
    tq = _tile(seq, 1024)
    tk = _tile(tq, 512)
    tm = _tile(m, 512)
    tm_mlp = _tile(m, 512)

    xf = x.reshape(m, D_MODEL)
    tables = _rope_tables(positions.reshape(m, 1), ts)
    row = lambda v: v[None, :]

    for l in range(depth):
        h, z_pool, z_conv, z_sgu, cq, ckv, kr = _inproj(xf, row(pre_mix_g[l]), _small_in_weight(w_in[l]), tm)
        y_pool = _pool(z_pool, pool_w[l].astype(BF16), row(pool_scale[l]), ts, seq)
        y_conv = _conv(z_conv, conv_w[l], row(conv_b[l]), row(conv_norm_g[l]), row(conv_norm_b[l]), ts_conv, seq)
        y_sgu = _sgu(z_sgu, row(sgu_norm_g[l]), row(sgu_norm_b[l]), sgu_w[l], sgu_b[l].T, ts)
        q, kn, kro, v = _mla_prep(cq, ckv, kr, tables, row(q_norm_g[l]), row(kv_norm_g[l]),
                                  _q_weight(w_uq[l]), _kv_weight(w_ukv[l]), ts)
        y_attn = _attention(q, kn, kro, v, batch, seq, tq, tk)
        merged = _merge(h, (y_pool, y_conv, y_sgu, y_attn), w_in[l][:, OFF_GATE:].astype(BF16),
                        (pool_proj[l].astype(BF16), conv_proj[l].astype(BF16), sgu_proj[l].astype(BF16),
                         attn_proj[l].astype(BF16)), tm, 512)
        x1, h2 = _outproj(merged, w_out[l].astype(BF16), xf, row(post_mix_g[l]), row(pre_mlp_g[l]), tm)
        xf = _mlp(h2, w_up[l].astype(BF16), w_down[l].astype(BF16), x1, row(post_mlp_g[l]), tm_mlp, 1024)
    return xf.reshape(batch, seq, D_MODEL)
```

```python
import functools

import jax
import jax.numpy as jnp
from jax import lax
from jax.experimental import pallas as pl
from jax.experimental.pallas import tpu as pltpu

F32 = jnp.float32
BF16 = jnp.bfloat16

D_MODEL = 2048
EPS = 1e-6
N_BRANCH = 4
D_FF = 4 * D_MODEL
POOL_DIM = 512
POOL_WINDOWS = (2, 4, 8, 16)
POOL_GDIM = POOL_DIM // len(POOL_WINDOWS)
CONV_DIM = 512
CONV_WIDTH = 31
SGU_DIM = 512
SGU_GROUPS = 4
SGU_GDIM = SGU_DIM // SGU_GROUPS
CHUNK = 128
MLA_HEADS = 8
Q_LORA = 512
KV_LORA = 512
QK_NOPE = 128
QK_ROPE = 64
V_DIM = 128
ROPE_THETA = 10000.0
OFF_KR = POOL_DIM + 2 * CONV_DIM + 2 * SGU_DIM + Q_LORA + KV_LORA
OFF_GATE = OFF_KR + QK_ROPE

LANES = 128
SUBLANES = 8
Q_HEAD_PAD = 2 * LANES
KR_PAD = LANES
N_SMALL = OFF_KR + KR_PAD
POOL_HALO = 16
CONV_HALO = 32
NEG_BIG = -1e30
LOG2_E = 1.4426950408889634
VMEM_CAP = 56 * 1024 * 1024

_SEGS = (POOL_DIM, 2 * CONV_DIM, 2 * SGU_DIM, Q_LORA, KV_LORA, KR_PAD)


def _params(semantics, vmem_bytes):
    return pltpu.CompilerParams(dimension_semantics=semantics,
                                vmem_limit_bytes=int(min(VMEM_CAP, max(vmem_bytes, 16 * 1024 * 1024))))


def _resident(shape):
    nd = len(shape)
    return pl.BlockSpec(shape, lambda *_: (0,) * nd, pipeline_mode=pl.Buffered(1))


def _rms(x, g):
    return x * lax.rsqrt(jnp.mean(x * x, axis=-1, keepdims=True) + EPS) * g


def _layernorm(x, g, b):
    mu = jnp.mean(x, axis=-1, keepdims=True)
    xc = x - mu
    var = jnp.mean(xc * xc, axis=-1, keepdims=True)
    return xc * lax.rsqrt(var + EPS) * g + b


def _dot(a, b):
    return jnp.dot(a, b, preferred_element_type=F32)


def _rope_table_kernel(pos_ref, freq_ref, c_ref, s1_ref, s2_ref):
    ang = pos_ref[...].astype(F32) * freq_ref[...]
    lane = lax.broadcasted_iota(jnp.int32, ang.shape, 1)
    cos = jnp.cos(ang)
    sin = jnp.sin(ang)
    half = QK_ROPE // 2
    c_ref[...] = jnp.where(lane < QK_ROPE, cos, 0.0)
    s1_ref[...] = jnp.where(lane < half, 0.0, jnp.where(lane < QK_ROPE, sin, 0.0))
    s2_ref[...] = jnp.where(lane < half, -sin, 0.0)


def _rope_tables(pos, tm):
    m = pos.shape[0]
    inv_freq = ROPE_THETA ** (-jnp.arange(0, QK_ROPE, 2, dtype=F32) / QK_ROPE)
    freq = jnp.concatenate([inv_freq, inv_freq, jnp.zeros((LANES - QK_ROPE,), F32)])[None, :]
    out = jax.ShapeDtypeStruct((m, LANES), F32)
    spec = pl.BlockSpec((tm, LANES), lambda i: (i, 0))
    return pl.pallas_call(
        _rope_table_kernel,
        grid=(m // tm,),
        in_specs=[pl.BlockSpec((tm, 1), lambda i: (i, 0)), pl.BlockSpec((1, LANES), lambda i: (0, 0))],
        out_specs=[spec, spec, spec],
        out_shape=[out, out, out],
        compiler_params=_params(("parallel",), 0),
        name="rope_tables",
    )(pos, freq)


def _rope(chunk, c, s1, s2):
    half = QK_ROPE // 2
    return (chunk * c + pltpu.roll(chunk, half, 1) * s1
            + pltpu.roll(chunk, LANES - half, 1) * s2)


def _inproj_kernel(x_ref, g_ref, w_ref, h_ref, *z_refs):
    h = _rms(x_ref[...], g_ref[...]).astype(BF16)
    h_ref[...] = h
    off = 0
    for z_ref, width in zip(z_refs, _SEGS):
        for c in range(0, width, 512):
            cw = min(512, width - c)
            z_ref[:, c:c + cw] = _dot(h, w_ref[:, off + c:off + c + cw]).astype(BF16)
        off += width


def _inproj(x, g, w_small, tm):
    m = x.shape[0]
    row = lambda i: (i, 0)
    out_shape = [jax.ShapeDtypeStruct((m, D_MODEL), BF16)]
    out_specs = [pl.BlockSpec((tm, D_MODEL), row)]
    for width in _SEGS:
        out_shape.append(jax.ShapeDtypeStruct((m, width), BF16))
        out_specs.append(pl.BlockSpec((tm, width), row))
    vmem = (2 * tm * D_MODEL * 4 + 2 * tm * D_MODEL * 2 + D_MODEL * N_SMALL * 2
            + 2 * tm * N_SMALL * 2 + 4 * tm * D_MODEL * 4)
    return pl.pallas_call(
        _inproj_kernel,
        grid=(m // tm,),
        in_specs=[pl.BlockSpec((tm, D_MODEL), row), _resident((1, D_MODEL)),
                  _resident((D_MODEL, N_SMALL))],
        out_specs=out_specs,
        out_shape=out_shape,
        compiler_params=_params(("parallel",), vmem),
        name="inproj",
    )(x, g, w_small)


def _pool_kernel(a_ref, halo_ref, w_ref, scale_ref, o_ref, ext_ref, *, ts, tiles_per_seq):
    i = pl.program_id(0)
    first = (i % tiles_per_seq) == 0
    a = a_ref[...].astype(F32)
    ext_ref[0:POOL_HALO, :] = jnp.where(first, 0.0, halo_ref[...].astype(F32))
    ext_ref[POOL_HALO:, :] = a
    t = (i % tiles_per_seq) * ts + lax.broadcasted_iota(jnp.int32, (ts, 1), 0)
    for gi, w in enumerate(POOL_WINDOWS):
        c0, c1 = gi * POOL_GDIM, (gi + 1) * POOL_GDIM
        win = ext_ref[POOL_HALO:POOL_HALO + ts, c0:c1]
        for d in range(1, w):
            win = win + ext_ref[POOL_HALO - d:POOL_HALO - d + ts, c0:c1]
        count = jnp.minimum(t + 1, w).astype(F32)
        pooled = (win / count - a[:, c0:c1]).astype(BF16)
        o_ref[:, c0:c1] = (_dot(pooled, w_ref[gi]) * scale_ref[:, c0:c1]).astype(BF16)


def _pool(z_pool, pool_w, pool_scale, ts, seq):
    m = z_pool.shape[0]
    hb = ts // POOL_HALO
    kern = functools.partial(_pool_kernel, ts=ts, tiles_per_seq=seq // ts)
    return pl.pallas_call(
        kern,
        grid=(m // ts,),
        in_specs=[pl.BlockSpec((ts, POOL_DIM), lambda i: (i, 0)),
                  pl.BlockSpec((POOL_HALO, POOL_DIM), lambda i: (jnp.maximum(i * hb - 1, 0), 0)),
                  _resident(pool_w.shape), _resident((1, POOL_DIM))],
        out_specs=pl.BlockSpec((ts, POOL_DIM), lambda i: (i, 0)),
        out_shape=jax.ShapeDtypeStruct((m, POOL_DIM), BF16),
        scratch_shapes=[pltpu.VMEM((ts + POOL_HALO, POOL_DIM), F32)],
        compiler_params=_params(("parallel",), 0),
        name="pool",
    )(z_pool, z_pool, pool_w, pool_scale)


_CONV_ROWS = 64


def _glu(c):
    c = c.astype(F32)
    return c[:, :CONV_DIM] * jax.nn.sigmoid(c[:, CONV_DIM:])


def _conv_kernel(c_ref, halo_ref, w_ref, b_ref, ng_ref, nb_ref, o_ref, ext_ref, y_ref, *, ts, tiles_per_seq):
    i = pl.program_id(0)
    first = (i % tiles_per_seq) == 0
    ext_ref[0, 0:CONV_HALO, :] = jnp.where(first, 0.0, _glu(halo_ref[...]))
    ext_ref[0, CONV_HALO:, :] = _glu(c_ref[...])
    span = ts + CONV_HALO - SUBLANES
    for p in range(1, SUBLANES):
        ext_ref[p, 0:span, :] = ext_ref[0, p:p + span, :]
    base = CONV_HALO - (CONV_WIDTH - 1)
    for r in range(0, ts, _CONV_ROWS):
        for c in range(0, CONV_DIM, LANES):
            acc = None
            for k in range(CONV_WIDTH):
                a, p = divmod(base + k, SUBLANES)
                u = SUBLANES * a + r
                term = ext_ref[p, u:u + _CONV_ROWS, c:c + LANES] * w_ref[k:k + 1, c:c + LANES]
                acc = term if acc is None else acc + term
            y_ref[r:r + _CONV_ROWS, c:c + LANES] = acc
    y = _layernorm(y_ref[...] + b_ref[...], ng_ref[...], nb_ref[...])
    o_ref[...] = (y * jax.nn.sigmoid(y)).astype(BF16)


def _conv(z_conv, conv_w, conv_b, norm_g, norm_b, ts, seq):
    m = z_conv.shape[0]
    hb = ts // CONV_HALO
    kern = functools.partial(_conv_kernel, ts=ts, tiles_per_seq=seq // ts)
    vec = _resident((1, CONV_DIM))
    return pl.pallas_call(
        kern,
        grid=(m // ts,),
        in_specs=[pl.BlockSpec((ts, 2 * CONV_DIM), lambda i: (i, 0)),
                  pl.BlockSpec((CONV_HALO, 2 * CONV_DIM), lambda i: (jnp.maximum(i * hb - 1, 0), 0)),
                  _resident((CONV_WIDTH, CONV_DIM)), vec, vec, vec],
        out_specs=pl.BlockSpec((ts, CONV_DIM), lambda i: (i, 0)),
        out_shape=jax.ShapeDtypeStruct((m, CONV_DIM), BF16),
        scratch_shapes=[pltpu.VMEM((SUBLANES, ts + CONV_HALO, CONV_DIM), F32), pltpu.VMEM((ts, CONV_DIM), F32)],
        compiler_params=_params(("parallel",), 0),
        name="conv",
    )(z_conv, z_conv, conv_w, conv_b, norm_g, norm_b)


def _gelu_tanh(x):
    return 0.5 * x * (1.0 + jnp.tanh(0.7978845608028654 * (x + 0.044715 * (x * x * x))))


def _sgu_kernel(z_ref, ng_ref, nb_ref, w_ref, bt_ref, o_ref, *, ts):
    z = _gelu_tanh(z_ref[...].astype(F32))
    u = z[:, :SGU_DIM]
    v = _layernorm(z[:, SGU_DIM:], ng_ref[...], nb_ref[...]).astype(BF16)
    row = lax.broadcasted_iota(jnp.int32, (CHUNK, CHUNK), 0)
    col = lax.broadcasted_iota(jnp.int32, (CHUNK, CHUNK), 1)
    n_chunks = ts // CHUNK
    for g in range(SGU_GROUPS):
        c0, c1 = g * SGU_GDIM, (g + 1) * SGU_GDIM
        w = jnp.where(col <= row, w_ref[g], 0.0).astype(BF16)
        rhs = jnp.concatenate([v[n * CHUNK:(n + 1) * CHUNK, c0:c1] for n in range(n_chunks)], axis=1)
        sp = _dot(w, rhs) + bt_ref[:, g:g + 1]
        for n in range(n_chunks):
            o_ref[n * CHUNK:(n + 1) * CHUNK, c0:c1] = (
                u[n * CHUNK:(n + 1) * CHUNK, c0:c1] * sp[:, n * SGU_GDIM:(n + 1) * SGU_GDIM]).astype(BF16)


def _sgu(z_sgu, norm_g, norm_b, sgu_w, sgu_bt, ts):
    m = z_sgu.shape[0]
    vec = _resident((1, SGU_DIM))
    return pl.pallas_call(
        functools.partial(_sgu_kernel, ts=ts),
        grid=(m // ts,),
        in_specs=[pl.BlockSpec((ts, 2 * SGU_DIM), lambda i: (i, 0)), vec, vec,
                  _resident(sgu_w.shape), _resident(sgu_bt.shape)],
        out_specs=pl.BlockSpec((ts, SGU_DIM), lambda i: (i, 0)),
        out_shape=jax.ShapeDtypeStruct((m, SGU_DIM), BF16),
        compiler_params=_params(("parallel",), 0),
        name="sgu",
    )(z_sgu, norm_g, norm_b, sgu_w, sgu_bt)


def _mla_prep_kernel(cq_ref, ckv_ref, kr_ref, c_ref, s1_ref, s2_ref, qg_ref, kvg_ref, wq_ref, wkv_ref,
                     q_ref, kn_ref, kro_ref, v_ref):
    c, s1, s2 = c_ref[...], s1_ref[...], s2_ref[...]
    scale = (QK_NOPE + QK_ROPE) ** -0.5 * LOG2_E
    qn = _rms(cq_ref[...].astype(F32), qg_ref[...]).astype(BF16)
    for h in range(MLA_HEADS):
        o = h * Q_HEAD_PAD
        qh = _dot(qn, wq_ref[:, o:o + Q_HEAD_PAD]) * scale
        q_ref[:, o:o + LANES] = qh[:, :LANES].astype(BF16)
        q_ref[:, o + LANES:o + Q_HEAD_PAD] = _rope(qh[:, LANES:], c, s1, s2).astype(BF16)
    kvn = _rms(ckv_ref[...].astype(F32), kvg_ref[...]).astype(BF16)
    width = MLA_HEADS * QK_NOPE
    for o in range(0, width, 512):
        kn_ref[:, o:o + 512] = _dot(kvn, wkv_ref[:, o:o + 512]).astype(BF16)
        v_ref[:, o:o + 512] = _dot(kvn, wkv_ref[:, width + o:width + o + 512]).astype(BF16)
    kro_ref[...] = _rope(kr_ref[...].astype(F32), c, s1, s2).astype(BF16)


def _mla_prep(cq, ckv, kr, tables, q_norm_g, kv_norm_g, wq, wkv, ts):
    m = cq.shape[0]
    row = lambda i: (i, 0)
    tab = pl.BlockSpec((ts, LANES), row)
    hq = MLA_HEADS * Q_HEAD_PAD
    hk = MLA_HEADS * QK_NOPE
    return pl.pallas_call(
        _mla_prep_kernel,
        grid=(m // ts,),
        in_specs=[pl.BlockSpec((ts, Q_LORA), row), pl.BlockSpec((ts, KV_LORA), row),
                  pl.BlockSpec((ts, KR_PAD), row), tab, tab, tab,
                  _resident((1, Q_LORA)), _resident((1, KV_LORA)),
                  _resident(wq.shape), _resident(wkv.shape)],
        out_specs=[pl.BlockSpec((ts, hq), row), pl.BlockSpec((ts, hk), row),
                   pl.BlockSpec((ts, KR_PAD), row), pl.BlockSpec((ts, hk), row)],
        out_shape=[jax.ShapeDtypeStruct((m, hq), BF16), jax.ShapeDtypeStruct((m, hk), BF16),
                   jax.ShapeDtypeStruct((m, KR_PAD), BF16), jax.ShapeDtypeStruct((m, hk), BF16)],
        compiler_params=_params(("parallel",), 32 * 1024 * 1024),
        name="mla_prep",
    )(cq, ckv, kr, *tables, q_norm_g, kv_norm_g, wq, wkv)


_ATTN_ROWS = 64


def _attn_kernel(q_ref, kn_ref, kr_ref, v_ref, o_ref, s_ref, p_ref, m_ref, l_ref, acc_ref, *, tq, tk):
    qi = pl.program_id(2)
    n_diag = tq // tk
    m_ref[...] = jnp.full(m_ref.shape, NEG_BIG, F32)
    l_ref[...] = jnp.zeros(l_ref.shape, F32)
    acc_ref[...] = jnp.zeros(acc_ref.shape, F32)

    def block(j, row_start, mask_shift):
        k0 = pl.multiple_of(j * tk, tk)
        k = jnp.concatenate([kn_ref[pl.ds(k0, tk), :], kr_ref[pl.ds(k0, tk), :]], axis=-1)
        s_ref[row_start:, :] = lax.dot_general(q_ref[row_start:, :], k, (((1,), (1,)), ((), ())),
                                               preferred_element_type=F32)
        for c in range(row_start, tq, _ATTN_ROWS):
            rows = slice(c, c + _ATTN_ROWS)
            s = s_ref[rows, :]
            if mask_shift is not None and mask_shift + tk - 1 > c:
                row = lax.broadcasted_iota(jnp.int32, s.shape, 0) + c
                col = lax.broadcasted_iota(jnp.int32, s.shape, 1) + mask_shift
                s = jnp.where(col <= row, s, NEG_BIG)
            m_prev = m_ref[rows, :]
            m_next = jnp.maximum(m_prev, jnp.max(s, axis=1, keepdims=True))
            alpha = jnp.exp2(m_prev - m_next)
            p = jnp.exp2(s - jnp.tile(m_next, (1, tk // LANES)))
            l_ref[rows, :] = alpha * l_ref[rows, :] + jnp.sum(p, axis=1, keepdims=True)
            m_ref[rows, :] = m_next
            acc_ref[rows, :] = alpha * acc_ref[rows, :]
            p_ref[rows, :] = p.astype(BF16)
        acc_ref[row_start:, :] += _dot(p_ref[row_start:, :], v_ref[pl.ds(k0, tk), :])

    def body(j, carry):
        block(j, 0, None)
        return carry

    lax.fori_loop(0, qi * n_diag, body, 0)
    for d in range(n_diag):
        block(qi * n_diag + d, d * tk, d * tk)
    o_ref[...] = (acc_ref[...] / l_ref[...]).astype(BF16)


def _attention(q, kn, kr, v, batch, seq, tq, tk):
    m = q.shape[0]
    nq = seq // tq
    kern = functools.partial(_attn_kernel, tq=tq, tk=tk)
    return pl.pallas_call(
        kern,
        grid=(batch, MLA_HEADS, nq),
        in_specs=[pl.BlockSpec((tq, Q_HEAD_PAD), lambda b, h, i: (b * nq + i, h)),
                  pl.BlockSpec((seq, QK_NOPE), lambda b, h, i: (b, h)),
                  pl.BlockSpec((seq, KR_PAD), lambda b, h, i: (b, 0)),
                  pl.BlockSpec((seq, V_DIM), lambda b, h, i: (b, h))],
        out_specs=pl.BlockSpec((tq, V_DIM), lambda b, h, i: (b * nq + i, h)),
        out_shape=jax.ShapeDtypeStruct((m, MLA_HEADS * V_DIM), BF16),
        scratch_shapes=[pltpu.VMEM((tq, tk), F32), pltpu.VMEM((tq, tk), BF16), pltpu.VMEM((tq, LANES), F32),
                        pltpu.VMEM((tq, LANES), F32), pltpu.VMEM((tq, V_DIM), F32)],
        compiler_params=_params(("parallel", "parallel", "arbitrary"), 32 * 1024 * 1024),
        name="attention",
    )(q, kn, kr, v)


def _merge_kernel(h_ref, bp_ref, bc_ref, bs_ref, ba_ref, g0_ref, g1_ref, g2_ref, g3_ref,
                  pp_ref, pc_ref, ps_ref, pa_ref, o_ref):
    h = h_ref[...]
    acc = None
    for b_ref, g_ref, p_ref in ((bp_ref, g0_ref, pp_ref), (bc_ref, g1_ref, pc_ref),
                                (bs_ref, g2_ref, ps_ref), (ba_ref, g3_ref, pa_ref)):
        term = jax.nn.sigmoid(_dot(h, g_ref[...])) * _dot(b_ref[...], p_ref[...])
        acc = term if acc is None else acc + term
    o_ref[...] = acc.astype(BF16)


def _merge(h, branches, w_gate, projs, tm, tn):
    m = h.shape[0]
    nj = D_MODEL // tn
    row = lambda i, j: (i, 0)
    in_specs = [pl.BlockSpec((tm, D_MODEL), row)]
    in_specs += [pl.BlockSpec((tm, b.shape[1]), row) for b in branches]
    in_specs += [pl.BlockSpec((D_MODEL, tn), functools.partial(lambda i, j, b: (0, b * nj + j), b=b))
                 for b in range(N_BRANCH)]
    in_specs += [pl.BlockSpec((p.shape[0], tn), lambda i, j: (0, j)) for p in projs]
    kin = sum(b.shape[1] for b in branches)
    vmem = 2 * 2 * (tm * D_MODEL + tm * kin + N_BRANCH * D_MODEL * tn + kin * tn + tm * tn) + 6 * tm * tn * 4
    return pl.pallas_call(
        _merge_kernel,
        grid=(m // tm, nj),
        in_specs=in_specs,
        out_specs=pl.BlockSpec((tm, tn), lambda i, j: (i, j)),
        out_shape=jax.ShapeDtypeStruct((m, D_MODEL), BF16),
        compiler_params=_params(("parallel", "arbitrary"), vmem),
        name="merge",
    )(h, *branches, w_gate, w_gate, w_gate, w_gate, *projs)


def _outproj_kernel(m_ref, w_ref, x_ref, gpost_ref, gpre_ref, x1_ref, h2_ref, y_ref):
    mg = m_ref[...]
    for c in range(0, D_MODEL, 512):
        y_ref[:, c:c + 512] = _dot(mg, w_ref[:, c:c + 512])
    x1 = x_ref[...] + _rms(y_ref[...], gpost_ref[...])
    x1_ref[...] = x1
    h2_ref[...] = _rms(x1, gpre_ref[...]).astype(BF16)


def _outproj(merged, w_out, x, g_post, g_pre, tm):
    m = x.shape[0]
    row = lambda i: (i, 0)
    vec = _resident((1, D_MODEL))
    vmem = 2 * tm * D_MODEL * (2 + 4 + 4 + 2) + D_MODEL * D_MODEL * 2 + 4 * tm * D_MODEL * 4
    return pl.pallas_call(
        _outproj_kernel,
        grid=(m // tm,),
        in_specs=[pl.BlockSpec((tm, D_MODEL), row), _resident((D_MODEL, D_MODEL)),
                  pl.BlockSpec((tm, D_MODEL), row), vec, vec],
        out_specs=[pl.BlockSpec((tm, D_MODEL), row), pl.BlockSpec((tm, D_MODEL), row)],
        out_shape=[jax.ShapeDtypeStruct((m, D_MODEL), F32), jax.ShapeDtypeStruct((m, D_MODEL), BF16)],
        scratch_shapes=[pltpu.VMEM((tm, D_MODEL), F32)],
        compiler_params=_params(("parallel",), vmem),
        name="outproj",
    )(merged, w_out, x, g_post, g_pre)


_MLP_COLS = 512


def _mlp_kernel(h_ref, wu_ref, wd_ref, x_ref, g_ref, o_ref, a_ref):
    f = pl.program_id(1)
    tf = a_ref.shape[1]

    @pl.when(f == 0)
    def _():
        o_ref[...] = jnp.zeros_like(o_ref)

    h = h_ref[...]
    for c in range(0, tf, _MLP_COLS):
        up = _dot(h, wu_ref[:, c:c + _MLP_COLS])
        a_ref[:, c:c + _MLP_COLS] = jnp.square(jnp.maximum(up, 0.0)).astype(BF16)
    a = a_ref[...]
    for c in range(0, D_MODEL, _MLP_COLS):
        o_ref[:, c:c + _MLP_COLS] += _dot(a, wd_ref[:, c:c + _MLP_COLS])

    @pl.when(f == pl.num_programs(1) - 1)
    def _():
        o_ref[...] = x_ref[...] + _rms(o_ref[...], g_ref[...])


def _mlp(h2, w_up, w_down, x1, g_post, tm, tf):
    m = x1.shape[0]
    row = lambda i, f: (i, 0)
    vmem = (tm * D_MODEL * (2 * 2 + 4 + 2 * 4) + 2 * 2 * 2 * D_MODEL * tf + tm * tf * 2
            + 3 * tm * _MLP_COLS * 4 + 4 * 1024 * 1024)
    return pl.pallas_call(
        _mlp_kernel,
        grid=(m // tm, D_FF // tf),
        in_specs=[pl.BlockSpec((tm, D_MODEL), row), pl.BlockSpec((D_MODEL, tf), lambda i, f: (0, f)),
                  pl.BlockSpec((tf, D_MODEL), lambda i, f: (f, 0)),
                  pl.BlockSpec((tm, D_MODEL), row, pipeline_mode=pl.Buffered(1)),
                  _resident((1, D_MODEL))],
        out_specs=pl.BlockSpec((tm, D_MODEL), row),
        out_shape=jax.ShapeDtypeStruct((m, D_MODEL), F32),
        scratch_shapes=[pltpu.VMEM((tm, tf), BF16)],
        compiler_params=_params(("parallel", "arbitrary"), vmem),
        name="mlp",
    )(h2, w_up, w_down, x1, g_post)


def _small_in_weight(w_in_l):
    pad = jnp.zeros((D_MODEL, KR_PAD - QK_ROPE), w_in_l.dtype)
    return jnp.concatenate([w_in_l[:, :OFF_GATE], pad], axis=1).astype(BF16)


def _q_weight(w_uq_l):
    w = w_uq_l.reshape(Q_LORA, MLA_HEADS, QK_NOPE + QK_ROPE)
    pad = jnp.zeros((Q_LORA, MLA_HEADS, Q_HEAD_PAD - QK_NOPE - QK_ROPE), w.dtype)
    return jnp.concatenate([w, pad], axis=2).reshape(Q_LORA, MLA_HEADS * Q_HEAD_PAD).astype(BF16)


def _kv_weight(w_ukv_l):
    w = w_ukv_l.reshape(KV_LORA, MLA_HEADS, QK_NOPE + V_DIM)
    k = w[:, :, :QK_NOPE].reshape(KV_LORA, MLA_HEADS * QK_NOPE)
    v = w[:, :, QK_NOPE:].reshape(KV_LORA, MLA_HEADS * V_DIM)
    return jnp.concatenate([k, v], axis=1).astype(BF16)


def _tile(n, want):
    t = min(n, want)
    assert n % t == 0, (n, t)
    return t


def kernel(x, positions, pre_mix_g, w_in, pool_w, pool_scale, pool_proj, conv_w, conv_b, conv_norm_g, conv_norm_b, conv_proj, sgu_norm_g, sgu_norm_b, sgu_w, sgu_b, sgu_proj, q_norm_g, w_uq, kv_norm_g, w_ukv, attn_proj, w_out, post_mix_g, pre_mlp_g, w_up, w_down, post_mlp_g):
    batch, seq, d = x.shape
    assert d == D_MODEL and seq % CHUNK == 0
    m = batch * seq
    depth = w_in.shape[0]
    ts = _tile(seq, 512)
    ts_conv = _tile(seq, 256)
    tq = _tile(seq, 2048)
    tk = _tile(tq, 512)
    tm = _tile(m, 512)
    tm_mlp = _tile(m, 1024)

    xf = x.reshape(m, D_MODEL)
    tables = _rope_tables(positions.reshape(m, 1), ts)
    row = lambda v: v[None, :]

    for l in range(depth):
        h, z_pool, z_conv, z_sgu, cq, ckv, kr = _inproj(xf, row(pre_mix_g[l]), _small_in_weight(w_in[l]), tm)
        y_pool = _pool(z_pool, pool_w[l].astype(BF16), row(pool_scale[l]), ts, seq)
        y_conv = _conv(z_conv, conv_w[l], row(conv_b[l]), row(conv_norm_g[l]), row(conv_norm_b[l]), ts_conv, seq)
        y_sgu = _sgu(z_sgu, row(sgu_norm_g[l]), row(sgu_norm_b[l]), sgu_w[l], sgu_b[l].T, ts)
        q, kn, kro, v = _mla_prep(cq, ckv, kr, tables, row(q_norm_g[l]), row(kv_norm_g[l]),
                                  _q_weight(w_uq[l]), _kv_weight(w_ukv[l]), ts)
        y_attn = _attention(q, kn, kro, v, batch, seq, tq, tk)
        merged = _merge(h, (y_pool, y_conv, y_sgu, y_attn), w_in[l][:, OFF_GATE:].astype(BF16),
                        (pool_proj[l].astype(BF16), conv_proj[l].astype(BF16), sgu_proj[l].astype(BF16),
                         attn_proj[l].astype(BF16)), tm, 512)
        x1, h2 = _outproj(merged, w_out[l].astype(BF16), xf, row(post_mix_g[l]), row(pre_mlp_g[l]), tm)
        xf = _mlp(h2, w_up[l].astype(BF16), w_down[l].astype(BF16), x1, row(post_mlp_g[l]), tm_mlp, 512)
    return xf.reshape(batch, seq, D_MODEL)
```

```python
import functools

import jax
import jax.numpy as jnp
from jax import lax
from jax.experimental import pallas as pl
from jax.experimental.pallas import tpu as pltpu

F32 = jnp.float32
BF16 = jnp.bfloat16

D_MODEL = 2048
EPS = 1e-6
N_BRANCH = 4
D_FF = 4 * D_MODEL
POOL_DIM = 512
POOL_WINDOWS = (2, 4, 8, 16)
POOL_GDIM = POOL_DIM // len(POOL_WINDOWS)
CONV_DIM = 512
CONV_WIDTH = 31
SGU_DIM = 512
SGU_GROUPS = 4
SGU_GDIM = SGU_DIM // SGU_GROUPS
CHUNK = 128
MLA_HEADS = 8
Q_LORA = 512
KV_LORA = 512
QK_NOPE = 128
QK_ROPE = 64
V_DIM = 128
ROPE_THETA = 10000.0
OFF_KR = POOL_DIM + 2 * CONV_DIM + 2 * SGU_DIM + Q_LORA + KV_LORA
OFF_GATE = OFF_KR + QK_ROPE

LANES = 128
SUBLANES = 8
Q_HEAD_PAD = 2 * LANES
KR_PAD = LANES
N_SMALL = OFF_KR + KR_PAD
POOL_HALO = 16
CONV_HALO = 32
NEG_BIG = -1e30
LOG2_E = 1.4426950408889634
VMEM_CAP = 56 * 1024 * 1024

_SEGS = (POOL_DIM, 2 * CONV_DIM, 2 * SGU_DIM, Q_LORA, KV_LORA, KR_PAD)


def _params(semantics, vmem_bytes):
    return pltpu.CompilerParams(dimension_semantics=semantics,
                                vmem_limit_bytes=int(min(VMEM_CAP, max(vmem_bytes, 16 * 1024 * 1024))))


def _resident(shape):
    nd = len(shape)
    return pl.BlockSpec(shape, lambda *_: (0,) * nd, pipeline_mode=pl.Buffered(1))


def _rms(x, g):
    return x * lax.rsqrt(jnp.mean(x * x, axis=-1, keepdims=True) + EPS) * g


def _layernorm(x, g, b):
    mu = jnp.mean(x, axis=-1, keepdims=True)
    xc = x - mu
    var = jnp.mean(xc * xc, axis=-1, keepdims=True)
    return xc * lax.rsqrt(var + EPS) * g + b


def _dot(a, b):
    return jnp.dot(a, b, preferred_element_type=F32)


def _rope_table_kernel(pos_ref, freq_ref, c_ref, s1_ref, s2_ref):
    ang = pos_ref[...].astype(F32) * freq_ref[...]
    lane = lax.broadcasted_iota(jnp.int32, ang.shape, 1)
    cos = jnp.cos(ang)
    sin = jnp.sin(ang)
    half = QK_ROPE // 2
    c_ref[...] = jnp.where(lane < QK_ROPE, cos, 0.0)
    s1_ref[...] = jnp.where(lane < half, 0.0, jnp.where(lane < QK_ROPE, sin, 0.0))
    s2_ref[...] = jnp.where(lane < half, -sin, 0.0)


def _rope_tables(pos, tm):
    m = pos.shape[0]
    inv_freq = ROPE_THETA ** (-jnp.arange(0, QK_ROPE, 2, dtype=F32) / QK_ROPE)
    freq = jnp.concatenate([inv_freq, inv_freq, jnp.zeros((LANES - QK_ROPE,), F32)])[None, :]
    out = jax.ShapeDtypeStruct((m, LANES), F32)
    spec = pl.BlockSpec((tm, LANES), lambda i: (i, 0))
    return pl.pallas_call(
        _rope_table_kernel,
        grid=(m // tm,),
        in_specs=[pl.BlockSpec((tm, 1), lambda i: (i, 0)), pl.BlockSpec((1, LANES), lambda i: (0, 0))],
        out_specs=[spec, spec, spec],
        out_shape=[out, out, out],
        compiler_params=_params(("parallel",), 0),
        name="rope_tables",
    )(pos, freq)


def _rope(chunk, c, s1, s2):
    half = QK_ROPE // 2
    return (chunk * c + pltpu.roll(chunk, half, 1) * s1
            + pltpu.roll(chunk, LANES - half, 1) * s2)


def _inproj_kernel(x_ref, g_ref, w_ref, h_ref, *z_refs):
    h = _rms(x_ref[...], g_ref[...]).astype(BF16)
    h_ref[...] = h
    off = 0
    for z_ref, width in zip(z_refs, _SEGS):
        for c in range(0, width, 512):
            cw = min(512, width - c)
            z_ref[:, c:c + cw] = _dot(h, w_ref[:, off + c:off + c + cw]).astype(BF16)
        off += width


def _inproj(x, g, w_small, tm):
    m = x.shape[0]
    row = lambda i: (i, 0)
    out_shape = [jax.ShapeDtypeStruct((m, D_MODEL), BF16)]
    out_specs = [pl.BlockSpec((tm, D_MODEL), row)]
    for width in _SEGS:
        out_shape.append(jax.ShapeDtypeStruct((m, width), BF16))
        out_specs.append(pl.BlockSpec((tm, width), row))
    vmem = (2 * tm * D_MODEL * 4 + 2 * tm * D_MODEL * 2 + D_MODEL * N_SMALL * 2
            + 2 * tm * N_SMALL * 2 + 4 * tm * D_MODEL * 4)
    return pl.pallas_call(
        _inproj_kernel,
        grid=(m // tm,),
        in_specs=[pl.BlockSpec((tm, D_MODEL), row), _resident((1, D_MODEL)),
                  _resident((D_MODEL, N_SMALL))],
        out_specs=out_specs,
        out_shape=out_shape,
        compiler_params=_params(("parallel",), vmem),
        name="inproj",
    )(x, g, w_small)


def _pool_kernel(a_ref, halo_ref, w_ref, scale_ref, o_ref, ext_ref, *, ts, tiles_per_seq):
    i = pl.program_id(0)
    first = (i % tiles_per_seq) == 0
    a = a_ref[...].astype(F32)
    ext_ref[0:POOL_HALO, :] = jnp.where(first, 0.0, halo_ref[...].astype(F32))
    ext_ref[POOL_HALO:, :] = a
    t = (i % tiles_per_seq) * ts + lax.broadcasted_iota(jnp.int32, (ts, 1), 0)
    for gi, w in enumerate(POOL_WINDOWS):
        c0, c1 = gi * POOL_GDIM, (gi + 1) * POOL_GDIM
        win = ext_ref[POOL_HALO:POOL_HALO + ts, c0:c1]
        for d in range(1, w):
            win = win + ext_ref[POOL_HALO - d:POOL_HALO - d + ts, c0:c1]
        count = jnp.minimum(t + 1, w).astype(F32)
        pooled = (win / count - a[:, c0:c1]).astype(BF16)
        o_ref[:, c0:c1] = (_dot(pooled, w_ref[gi]) * scale_ref[:, c0:c1]).astype(BF16)


def _pool(z_pool, pool_w, pool_scale, ts, seq):
    m = z_pool.shape[0]
    hb = ts // POOL_HALO
    kern = functools.partial(_pool_kernel, ts=ts, tiles_per_seq=seq // ts)
    return pl.pallas_call(
        kern,
        grid=(m // ts,),
        in_specs=[pl.BlockSpec((ts, POOL_DIM), lambda i: (i, 0)),
                  pl.BlockSpec((POOL_HALO, POOL_DIM), lambda i: (jnp.maximum(i * hb - 1, 0), 0)),
                  _resident(pool_w.shape), _resident((1, POOL_DIM))],
        out_specs=pl.BlockSpec((ts, POOL_DIM), lambda i: (i, 0)),
        out_shape=jax.ShapeDtypeStruct((m, POOL_DIM), BF16),
        scratch_shapes=[pltpu.VMEM((ts + POOL_HALO, POOL_DIM), F32)],
        compiler_params=_params(("parallel",), 0),
        name="pool",
    )(z_pool, z_pool, pool_w, pool_scale)


_CONV_ROWS = 64


def _glu(c):
    c = c.astype(F32)
    return c[:, :CONV_DIM] * jax.nn.sigmoid(c[:, CONV_DIM:])


def _conv_kernel(c_ref, halo_ref, w_ref, b_ref, ng_ref, nb_ref, o_ref, ext_ref, y_ref, *, ts, tiles_per_seq):
    i = pl.program_id(0)
    first = (i % tiles_per_seq) == 0
    ext_ref[0, 0:CONV_HALO, :] = jnp.where(first, 0.0, _glu(halo_ref[...]))
    ext_ref[0, CONV_HALO:, :] = _glu(c_ref[...])
    span = ts + CONV_HALO - SUBLANES
    for p in range(1, SUBLANES):
        ext_ref[p, 0:span, :] = ext_ref[0, p:p + span, :]
    base = CONV_HALO - (CONV_WIDTH - 1)
    for r in range(0, ts, _CONV_ROWS):
        for c in range(0, CONV_DIM, LANES):
            acc = None
            for k in range(CONV_WIDTH):
                a, p = divmod(base + k, SUBLANES)
                u = SUBLANES * a + r
                term = ext_ref[p, u:u + _CONV_ROWS, c:c + LANES] * w_ref[k:k + 1, c:c + LANES]
                acc = term if acc is None else acc + term
            y_ref[r:r + _CONV_ROWS, c:c + LANES] = acc
    y = _layernorm(y_ref[...] + b_ref[...], ng_ref[...], nb_ref[...])
    o_ref[...] = (y * jax.nn.sigmoid(y)).astype(BF16)


def _conv(z_conv, conv_w, conv_b, norm_g, norm_b, ts, seq):
    m = z_conv.shape[0]
    hb = ts // CONV_HALO
    kern = functools.partial(_conv_kernel, ts=ts, tiles_per_seq=seq // ts)
    vec = _resident((1, CONV_DIM))
    return pl.pallas_call(
        kern,
        grid=(m // ts,),
        in_specs=[pl.BlockSpec((ts, 2 * CONV_DIM), lambda i: (i, 0)),
                  pl.BlockSpec((CONV_HALO, 2 * CONV_DIM), lambda i: (jnp.maximum(i * hb - 1, 0), 0)),
                  _resident((CONV_WIDTH, CONV_DIM)), vec, vec, vec],
        out_specs=pl.BlockSpec((ts, CONV_DIM), lambda i: (i, 0)),
        out_shape=jax.ShapeDtypeStruct((m, CONV_DIM), BF16),
        scratch_shapes=[pltpu.VMEM((SUBLANES, ts + CONV_HALO, CONV_DIM), F32), pltpu.VMEM((ts, CONV_DIM), F32)],
        compiler_params=_params(("parallel",), 0),
        name="conv",
    )(z_conv, z_conv, conv_w, conv_b, norm_g, norm_b)


def _gelu_tanh(x):
    return 0.5 * x * (1.0 + jnp.tanh(0.7978845608028654 * (x + 0.044715 * (x * x * x))))


def _sgu_kernel(z_ref, ng_ref, nb_ref, w_ref, bt_ref, o_ref, *, ts):
    z = _gelu_tanh(z_ref[...].astype(F32))
    u = z[:, :SGU_DIM]
    v = _layernorm(z[:, SGU_DIM:], ng_ref[...], nb_ref[...]).astype(BF16)
    row = lax.broadcasted_iota(jnp.int32, (CHUNK, CHUNK), 0)
    col = lax.broadcasted_iota(jnp.int32, (CHUNK, CHUNK), 1)
    n_chunks = ts // CHUNK
    for g in range(SGU_GROUPS):
        c0, c1 = g * SGU_GDIM, (g + 1) * SGU_GDIM
        w = jnp.where(col <= row, w_ref[g], 0.0).astype(BF16)
        rhs = jnp.concatenate([v[n * CHUNK:(n + 1) * CHUNK, c0:c1] for n in range(n_chunks)], axis=1)
        sp = _dot(w, rhs) + bt_ref[:, g:g + 1]
        for n in range(n_chunks):
            o_ref[n * CHUNK:(n + 1) * CHUNK, c0:c1] = (
                u[n * CHUNK:(n + 1) * CHUNK, c0:c1] * sp[:, n * SGU_GDIM:(n + 1) * SGU_GDIM]).astype(BF16)


def _sgu(z_sgu, norm_g, norm_b, sgu_w, sgu_bt, ts):
    m = z_sgu.shape[0]
    vec = _resident((1, SGU_DIM))
    return pl.pallas_call(
        functools.partial(_sgu_kernel, ts=ts),
        grid=(m // ts,),
        in_specs=[pl.BlockSpec((ts, 2 * SGU_DIM), lambda i: (i, 0)), vec, vec,
                  _resident(sgu_w.shape), _resident(sgu_bt.shape)],
        out_specs=pl.BlockSpec((ts, SGU_DIM), lambda i: (i, 0)),
        out_shape=jax.ShapeDtypeStruct((m, SGU_DIM), BF16),
        compiler_params=_params(("parallel",), 0),
        name="sgu",
    )(z_sgu, norm_g, norm_b, sgu_w, sgu_bt)


def _mla_prep_kernel(cq_ref, ckv_ref, kr_ref, c_ref, s1_ref, s2_ref, qg_ref, kvg_ref, wq_ref, wkv_ref,
                     q_ref, kn_ref, kro_ref, v_ref):
    c, s1, s2 = c_ref[...], s1_ref[...], s2_ref[...]
    scale = (QK_NOPE + QK_ROPE) ** -0.5 * LOG2_E
    qn = _rms(cq_ref[...].astype(F32), qg_ref[...]).astype(BF16)
    for h in range(MLA_HEADS):
        o = h * Q_HEAD_PAD
        qh = _dot(qn, wq_ref[:, o:o + Q_HEAD_PAD]) * scale
        q_ref[:, o:o + LANES] = qh[:, :LANES].astype(BF16)
        q_ref[:, o + LANES:o + Q_HEAD_PAD] = _rope(qh[:, LANES:], c, s1, s2).astype(BF16)
    kvn = _rms(ckv_ref[...].astype(F32), kvg_ref[...]).astype(BF16)
    width = MLA_HEADS * QK_NOPE
    for o in range(0, width, 512):
        kn_ref[:, o:o + 512] = _dot(kvn, wkv_ref[:, o:o + 512]).astype(BF16)
        v_ref[:, o:o + 512] = _dot(kvn, wkv_ref[:, width + o:width + o + 512]).astype(BF16)
    kro_ref[...] = _rope(kr_ref[...].astype(F32), c, s1, s2).astype(BF16)


def _mla_prep(cq, ckv, kr, tables, q_norm_g, kv_norm_g, wq, wkv, ts):
    m = cq.shape[0]
    row = lambda i: (i, 0)
    tab = pl.BlockSpec((ts, LANES), row)
    hq = MLA_HEADS * Q_HEAD_PAD
    hk = MLA_HEADS * QK_NOPE
    return pl.pallas_call(
        _mla_prep_kernel,
        grid=(m // ts,),
        in_specs=[pl.BlockSpec((ts, Q_LORA), row), pl.BlockSpec((ts, KV_LORA), row),
                  pl.BlockSpec((ts, KR_PAD), row), tab, tab, tab,
                  _resident((1, Q_LORA)), _resident((1, KV_LORA)),
                  _resident(wq.shape), _resident(wkv.shape)],
        out_specs=[pl.BlockSpec((ts, hq), row), pl.BlockSpec((ts, hk), row),
                   pl.BlockSpec((ts, KR_PAD), row), pl.BlockSpec((ts, hk), row)],
        out_shape=[jax.ShapeDtypeStruct((m, hq), BF16), jax.ShapeDtypeStruct((m, hk), BF16),
                   jax.ShapeDtypeStruct((m, KR_PAD), BF16), jax.ShapeDtypeStruct((m, hk), BF16)],
        compiler_params=_params(("parallel",), 32 * 1024 * 1024),
        name="mla_prep",
    )(cq, ckv, kr, *tables, q_norm_g, kv_norm_g, wq, wkv)


_ATTN_ROWS = 64


def _attn_kernel(q_ref, kn_ref, kr_ref, v_ref, o_ref, s_ref, p_ref, m_ref, l_ref, acc_ref, *, tq, tk):
    qi = pl.program_id(2)
    n_diag = tq // tk
    m_ref[...] = jnp.full(m_ref.shape, NEG_BIG, F32)
    l_ref[...] = jnp.zeros(l_ref.shape, F32)
    acc_ref[...] = jnp.zeros(acc_ref.shape, F32)

    def block(j, row_start, mask_shift):
        k0 = pl.multiple_of(j * tk, tk)
        k = jnp.concatenate([kn_ref[pl.ds(k0, tk), :], kr_ref[pl.ds(k0, tk), :]], axis=-1)
        s_ref[row_start:, :] = lax.dot_general(q_ref[row_start:, :], k, (((1,), (1,)), ((), ())),
                                               preferred_element_type=F32)
        for c in range(row_start, tq, _ATTN_ROWS):
            rows = slice(c, c + _ATTN_ROWS)
            s = s_ref[rows, :]
            if mask_shift is not None and mask_shift + tk - 1 > c:
                row = lax.broadcasted_iota(jnp.int32, s.shape, 0) + c
                col = lax.broadcasted_iota(jnp.int32, s.shape, 1) + mask_shift
                s = jnp.where(col <= row, s, NEG_BIG)
            m_prev = m_ref[rows, :]
            m_next = jnp.maximum(m_prev, jnp.max(s, axis=1, keepdims=True))
            alpha = jnp.exp2(m_prev - m_next)
            p = jnp.exp2(s - jnp.tile(m_next, (1, tk // LANES)))
            l_ref[rows, :] = alpha * l_ref[rows, :] + jnp.sum(p, axis=1, keepdims=True)
            m_ref[rows, :] = m_next
            acc_ref[rows, :] = alpha * acc_ref[rows, :]
            p_ref[rows, :] = p.astype(BF16)
        acc_ref[row_start:, :] += _dot(p_ref[row_start:, :], v_ref[pl.ds(k0, tk), :])

    def body(j, carry):
        block(j, 0, None)
        return carry

    lax.fori_loop(0, qi * n_diag, body, 0)
    for d in range(n_diag):
        block(qi * n_diag + d, d * tk, d * tk)
    o_ref[...] = (acc_ref[...] / l_ref[...]).astype(BF16)


def _attention(q, kn, kr, v, batch, seq, tq, tk):
    m = q.shape[0]
    nq = seq // tq
    kern = functools.partial(_attn_kernel, tq=tq, tk=tk)
    vmem = (2 * 2 * (tq * (Q_HEAD_PAD + V_DIM) + seq * (QK_NOPE + KR_PAD + V_DIM))
            + tq * tk * (4 + 2) + tq * (2 * LANES + V_DIM) * 4 + 12 * 1024 * 1024)
    return pl.pallas_call(
        kern,
        grid=(batch, MLA_HEADS, nq),
        in_specs=[pl.BlockSpec((tq, Q_HEAD_PAD), lambda b, h, i: (b * nq + i, h)),
                  pl.BlockSpec((seq, QK_NOPE), lambda b, h, i: (b, h)),
                  pl.BlockSpec((seq, KR_PAD), lambda b, h, i: (b, 0)),
                  pl.BlockSpec((seq, V_DIM), lambda b, h, i: (b, h))],
        out_specs=pl.BlockSpec((tq, V_DIM), lambda b, h, i: (b * nq + i, h)),
        out_shape=jax.ShapeDtypeStruct((m, MLA_HEADS * V_DIM), BF16),
        scratch_shapes=[pltpu.VMEM((tq, tk), F32), pltpu.VMEM((tq, tk), BF16), pltpu.VMEM((tq, LANES), F32),
                        pltpu.VMEM((tq, LANES), F32), pltpu.VMEM((tq, V_DIM), F32)],
        compiler_params=_params(("parallel", "parallel", "arbitrary"), vmem),
        name="attention",
    )(q, kn, kr, v)


def _merge_kernel(h_ref, bp_ref, bc_ref, bs_ref, ba_ref, g0_ref, g1_ref, g2_ref, g3_ref,
                  pp_ref, pc_ref, ps_ref, pa_ref, o_ref):
    h = h_ref[...]
    acc = None
    for b_ref, g_ref, p_ref in ((bp_ref, g0_ref, pp_ref), (bc_ref, g1_ref, pc_ref),
                                (bs_ref, g2_ref, ps_ref), (ba_ref, g3_ref, pa_ref)):
        term = jax.nn.sigmoid(_dot(h, g_ref[...])) * _dot(b_ref[...], p_ref[...])
        acc = term if acc is None else acc + term
    o_ref[...] = acc.astype(BF16)


def _merge(h, branches, w_gate, projs, tm, tn):
    m = h.shape[0]
    nj = D_MODEL // tn
    row = lambda i, j: (i, 0)
    in_specs = [pl.BlockSpec((tm, D_MODEL), row)]
    in_specs += [pl.BlockSpec((tm, b.shape[1]), row) for b in branches]
    in_specs += [pl.BlockSpec((D_MODEL, tn), functools.partial(lambda i, j, b: (0, b * nj + j), b=b))
                 for b in range(N_BRANCH)]
    in_specs += [pl.BlockSpec((p.shape[0], tn), lambda i, j: (0, j)) for p in projs]
    kin = sum(b.shape[1] for b in branches)
    vmem = 2 * 2 * (tm * D_MODEL + tm * kin + N_BRANCH * D_MODEL * tn + kin * tn + tm * tn) + 6 * tm * tn * 4
    return pl.pallas_call(
        _merge_kernel,
        grid=(m // tm, nj),
        in_specs=in_specs,
        out_specs=pl.BlockSpec((tm, tn), lambda i, j: (i, j)),
        out_shape=jax.ShapeDtypeStruct((m, D_MODEL), BF16),
        compiler_params=_params(("parallel", "arbitrary"), vmem),
        name="merge",
    )(h, *branches, w_gate, w_gate, w_gate, w_gate, *projs)


def _outproj_kernel(m_ref, w_ref, x_ref, gpost_ref, gpre_ref, x1_ref, h2_ref, y_ref):
    mg = m_ref[...]
    for c in range(0, D_MODEL, 512):
        y_ref[:, c:c + 512] = _dot(mg, w_ref[:, c:c + 512])
    x1 = x_ref[...] + _rms(y_ref[...], gpost_ref[...])
    x1_ref[...] = x1
    h2_ref[...] = _rms(x1, gpre_ref[...]).astype(BF16)


def _outproj(merged, w_out, x, g_post, g_pre, tm):
    m = x.shape[0]
    row = lambda i: (i, 0)
    vec = _resident((1, D_MODEL))
    vmem = 2 * tm * D_MODEL * (2 + 4 + 4 + 2) + D_MODEL * D_MODEL * 2 + 4 * tm * D_MODEL * 4
    return pl.pallas_call(
        _outproj_kernel,
        grid=(m // tm,),
        in_specs=[pl.BlockSpec((tm, D_MODEL), row), _resident((D_MODEL, D_MODEL)),
                  pl.BlockSpec((tm, D_MODEL), row), vec, vec],
        out_specs=[pl.BlockSpec((tm, D_MODEL), row), pl.BlockSpec((tm, D_MODEL), row)],
        out_shape=[jax.ShapeDtypeStruct((m, D_MODEL), F32), jax.ShapeDtypeStruct((m, D_MODEL), BF16)],
        scratch_shapes=[pltpu.VMEM((tm, D_MODEL), F32)],
        compiler_params=_params(("parallel",), vmem),
        name="outproj",
    )(merged, w_out, x, g_post, g_pre)


_MLP_COLS = 512


def _mlp_kernel(h_ref, wu_ref, wd_ref, xs_ref, g_ref, o_ref, a_ref, x_ref):
    f = pl.program_id(1)
    tf = a_ref.shape[1]
    rows = xs_ref.shape[0]
    x_ref[pl.ds(pl.multiple_of(f * rows, rows), rows), :] = xs_ref[...]

    @pl.when(f == 0)
    def _():
        o_ref[...] = jnp.zeros_like(o_ref)

    h = h_ref[...]
    for c in range(0, tf, _MLP_COLS):
        up = _dot(h, wu_ref[:, c:c + _MLP_COLS])
        a_ref[:, c:c + _MLP_COLS] = jnp.square(jnp.maximum(up, 0.0)).astype(BF16)
    a = a_ref[...]
    for c in range(0, D_MODEL, _MLP_COLS):
        o_ref[:, c:c + _MLP_COLS] += _dot(a, wd_ref[:, c:c + _MLP_COLS])

    @pl.when(f == pl.num_programs(1) - 1)
    def _():
        o_ref[...] = x_ref[...] + _rms(o_ref[...], g_ref[...])


def _mlp(h2, w_up, w_down, x1, g_post, tm, tf):
    m = x1.shape[0]
    row = lambda i, f: (i, 0)
    nf = D_FF // tf
    rows = tm // nf
    assert tm % nf == 0 and rows % SUBLANES == 0
    vmem = (tm * D_MODEL * (2 * 2 + 4 + 2 * 4) + 2 * 2 * 2 * D_MODEL * tf + tm * tf * 2
            + 3 * tm * _MLP_COLS * 4 + 4 * 1024 * 1024)
    return pl.pallas_call(
        _mlp_kernel,
        grid=(m // tm, nf),
        in_specs=[pl.BlockSpec((tm, D_MODEL), row), pl.BlockSpec((D_MODEL, tf), lambda i, f: (0, f)),
                  pl.BlockSpec((tf, D_MODEL), lambda i, f: (f, 0)),
                  pl.BlockSpec((rows, D_MODEL), lambda i, f: (i * nf + f, 0)),
                  _resident((1, D_MODEL))],
        out_specs=pl.BlockSpec((tm, D_MODEL), row),
        out_shape=jax.ShapeDtypeStruct((m, D_MODEL), F32),
        scratch_shapes=[pltpu.VMEM((tm, tf), BF16), pltpu.VMEM((tm, D_MODEL), F32)],
        compiler_params=_params(("parallel", "arbitrary"), vmem),
        name="mlp",
    )(h2, w_up, w_down, x1, g_post)


def _small_in_weight(w_in_l):
    pad = jnp.zeros((D_MODEL, KR_PAD - QK_ROPE), w_in_l.dtype)
    return jnp.concatenate([w_in_l[:, :OFF_GATE], pad], axis=1).astype(BF16)


def _q_weight(w_uq_l):
    w = w_uq_l.reshape(Q_LORA, MLA_HEADS, QK_NOPE + QK_ROPE)
    pad = jnp.zeros((Q_LORA, MLA_HEADS, Q_HEAD_PAD - QK_NOPE - QK_ROPE), w.dtype)
    return jnp.concatenate([w, pad], axis=2).reshape(Q_LORA, MLA_HEADS * Q_HEAD_PAD).astype(BF16)


def _kv_weight(w_ukv_l):
    w = w_ukv_l.reshape(KV_LORA, MLA_HEADS, QK_NOPE + V_DIM)
    k = w[:, :, :QK_NOPE].reshape(KV_LORA, MLA_HEADS * QK_NOPE)
    v = w[:, :, QK_NOPE:].reshape(KV_LORA, MLA_HEADS * V_DIM)
    return jnp.concatenate([k, v], axis=1).astype(BF16)


def _tile(n, want):
    t = min(n, want)
    assert n % t == 0, (n, t)
    return t


def kernel(x, positions, pre_mix_g, w_in, pool_w, pool_scale, pool_proj, conv_w, conv_b, conv_norm_g, conv_norm_b, conv_proj, sgu_norm_g, sgu_norm_b, sgu_w, sgu_b, sgu_proj, q_norm_g, w_uq, kv_norm_g, w_ukv, attn_proj, w_out, post_mix_g, pre_mlp_g, w_up, w_down, post_mlp_g):
    batch, seq, d = x.shape
    assert d == D_MODEL and seq % CHUNK == 0
    m = batch * seq
    depth = w_in.shape[0]
    ts = _tile(seq, 512)
    ts_conv = _tile(seq, 256)
    tq = _tile(seq, 4096)
    tk = _tile(tq, 512)
    tm = _tile(m, 512)
    tm_mlp = _tile(m, 1024)

    xf = x.reshape(m, D_MODEL)
    tables = _rope_tables(positions.reshape(m, 1), ts)
    row = lambda v: v[None, :]

    for l in range(depth):
        h, z_pool, z_conv, z_sgu, cq, ckv, kr = _inproj(xf, row(pre_mix_g[l]), _small_in_weight(w_in[l]), tm)
        y_pool = _pool(z_pool, pool_w[l].astype(BF16), row(pool_scale[l]), ts, seq)
        y_conv = _conv(z_conv, conv_w[l], row(conv_b[l]), row(conv_norm_g[l]), row(conv_norm_b[l]), ts_conv, seq)
        y_sgu = _sgu(z_sgu, row(sgu_norm_g[l]), row(sgu_norm_b[l]), sgu_w[l], sgu_b[l].T, ts)
        q, kn, kro, v = _mla_prep(cq, ckv, kr, tables, row(q_norm_g[l]), row(kv_norm_g[l]),
                                  _q_weight(w_uq[l]), _kv_weight(w_ukv[l]), ts)
        y_attn = _attention(q, kn, kro, v, batch, seq, tq, tk)
        merged = _merge(h, (y_pool, y_conv, y_sgu, y_attn), w_in[l][:, OFF_GATE:].astype(BF16),
                        (pool_proj[l].astype(BF16), conv_proj[l].astype(BF16), sgu_proj[l].astype(BF16),
                         attn_proj[l].astype(BF16)), tm_mlp, 512)
        x1, h2 = _outproj(merged, w_out[l].astype(BF16), xf, row(post_mix_g[l]), row(pre_mlp_g[l]), tm)
        xf = _mlp(h2, w_up[l].astype(BF16), w_down[l].astype(BF16), x1, row(post_mlp_g[l]), tm_mlp, 512)
    return xf.reshape(batch, seq, D_MODEL)
```

```python
import functools

import jax
import jax.numpy as jnp
from jax import lax
from jax.experimental import pallas as pl
from jax.experimental.pallas import tpu as pltpu

F32 = jnp.float32
BF16 = jnp.bfloat16

D_MODEL = 2048
EPS = 1e-6
N_BRANCH = 4
D_FF = 4 * D_MODEL
POOL_DIM = 512
POOL_WINDOWS = (2, 4, 8, 16)
POOL_GDIM = POOL_DIM // len(POOL_WINDOWS)
CONV_DIM = 512
CONV_WIDTH = 31
SGU_DIM = 512
SGU_GROUPS = 4
SGU_GDIM = SGU_DIM // SGU_GROUPS
CHUNK = 128
MLA_HEADS = 8
Q_LORA = 512
KV_LORA = 512
QK_NOPE = 128
QK_ROPE = 64
V_DIM = 128
ROPE_THETA = 10000.0
OFF_KR = POOL_DIM + 2 * CONV_DIM + 2 * SGU_DIM + Q_LORA + KV_LORA
OFF_GATE = OFF_KR + QK_ROPE

LANES = 128
SUBLANES = 8
Q_HEAD_PAD = 2 * LANES
KR_PAD = LANES
N_SMALL = OFF_KR + KR_PAD
POOL_HALO = 16
CONV_HALO = 32
NEG_BIG = -1e30
LOG2_E = 1.4426950408889634
VMEM_CAP = 56 * 1024 * 1024

_SEGS = (POOL_DIM, 2 * CONV_DIM, 2 * SGU_DIM, Q_LORA, KV_LORA, KR_PAD)


def _params(semantics, vmem_bytes):
    return pltpu.CompilerParams(dimension_semantics=semantics,
                                vmem_limit_bytes=int(min(VMEM_CAP, max(vmem_bytes, 16 * 1024 * 1024))))


def _resident(shape):
    nd = len(shape)
    return pl.BlockSpec(shape, lambda *_: (0,) * nd, pipeline_mode=pl.Buffered(1))


def _layer_resident(shape, l):
    nd = len(shape)
    return pl.BlockSpec((None,) + tuple(shape), lambda *_: (l,) + (0,) * nd, pipeline_mode=pl.Buffered(1))


def _rms(x, g):
    return x * lax.rsqrt(jnp.mean(x * x, axis=-1, keepdims=True) + EPS) * g


def _layernorm(x, g, b):
    mu = jnp.mean(x, axis=-1, keepdims=True)
    xc = x - mu
    var = jnp.mean(xc * xc, axis=-1, keepdims=True)
    return xc * lax.rsqrt(var + EPS) * g + b


def _dot(a, b):
    return jnp.dot(a, b, preferred_element_type=F32)


def _rope_table_kernel(pos_ref, freq_ref, c_ref, s1_ref, s2_ref):
    ang = pos_ref[...].astype(F32) * freq_ref[...]
    lane = lax.broadcasted_iota(jnp.int32, ang.shape, 1)
    cos = jnp.cos(ang)
    sin = jnp.sin(ang)
    half = QK_ROPE // 2
    c_ref[...] = jnp.where(lane < QK_ROPE, cos, 0.0)
    s1_ref[...] = jnp.where(lane < half, 0.0, jnp.where(lane < QK_ROPE, sin, 0.0))
    s2_ref[...] = jnp.where(lane < half, -sin, 0.0)


def _rope_tables(pos, tm):
    m = pos.shape[0]
    inv_freq = ROPE_THETA ** (-jnp.arange(0, QK_ROPE, 2, dtype=F32) / QK_ROPE)
    freq = jnp.concatenate([inv_freq, inv_freq, jnp.zeros((LANES - QK_ROPE,), F32)])[None, :]
    out = jax.ShapeDtypeStruct((m, LANES), F32)
    spec = pl.BlockSpec((tm, LANES), lambda i: (i, 0))
    return pl.pallas_call(
        _rope_table_kernel,
        grid=(m // tm,),
        in_specs=[pl.BlockSpec((tm, 1), lambda i: (i, 0)), pl.BlockSpec((1, LANES), lambda i: (0, 0))],
        out_specs=[spec, spec, spec],
        out_shape=[out, out, out],
        compiler_params=_params(("parallel",), 0),
        name="rope_tables",
    )(pos, freq)


def _rope(chunk, c, s1, s2):
    half = QK_ROPE // 2
    return (chunk * c + pltpu.roll(chunk, half, 1) * s1
            + pltpu.roll(chunk, LANES - half, 1) * s2)


def _inproj_kernel(x_ref, g_ref, w_ref, h_ref, *z_refs):
    h = _rms(x_ref[...], g_ref[...]).astype(BF16)
    h_ref[...] = h
    off = 0
    for z_ref, width in zip(z_refs, _SEGS):
        for c in range(0, width, 512):
            cw = min(512, width - c)
            z_ref[:, c:c + cw] = _dot(h, w_ref[:, off + c:off + c + cw]).astype(BF16)
        off += width


def _inproj(x, g, w_small, l, tm):
    m = x.shape[0]
    row = lambda i: (i, 0)
    out_shape = [jax.ShapeDtypeStruct((m, D_MODEL), BF16)]
    out_specs = [pl.BlockSpec((tm, D_MODEL), row)]
    for width in _SEGS:
        out_shape.append(jax.ShapeDtypeStruct((m, width), BF16))
        out_specs.append(pl.BlockSpec((tm, width), row))
    vmem = (2 * tm * D_MODEL * 4 + 2 * tm * D_MODEL * 2 + D_MODEL * N_SMALL * 2
            + 2 * tm * N_SMALL * 2 + 4 * tm * D_MODEL * 4)
    return pl.pallas_call(
        _inproj_kernel,
        grid=(m // tm,),
        in_specs=[pl.BlockSpec((tm, D_MODEL), row), _resident((1, D_MODEL)),
                  _layer_resident((D_MODEL, N_SMALL), l)],
        out_specs=out_specs,
        out_shape=out_shape,
        compiler_params=_params(("parallel",), vmem),
        name="inproj",
    )(x, g, w_small)


def _pool_kernel(a_ref, halo_ref, w_ref, scale_ref, o_ref, ext_ref, *, ts, tiles_per_seq):
    i = pl.program_id(0)
    first = (i % tiles_per_seq) == 0
    a = a_ref[...].astype(F32)
    ext_ref[0:POOL_HALO, :] = jnp.where(first, 0.0, halo_ref[...].astype(F32))
    ext_ref[POOL_HALO:, :] = a
    t = (i % tiles_per_seq) * ts + lax.broadcasted_iota(jnp.int32, (ts, 1), 0)
    for gi, w in enumerate(POOL_WINDOWS):
        c0, c1 = gi * POOL_GDIM, (gi + 1) * POOL_GDIM
        win = ext_ref[POOL_HALO:POOL_HALO + ts, c0:c1]
        for d in range(1, w):
            win = win + ext_ref[POOL_HALO - d:POOL_HALO - d + ts, c0:c1]
        count = jnp.minimum(t + 1, w).astype(F32)
        pooled = (win / count - a[:, c0:c1]).astype(BF16)
        o_ref[:, c0:c1] = (_dot(pooled, w_ref[gi]) * scale_ref[:, c0:c1]).astype(BF16)


def _pool(z_pool, pool_w, l, pool_scale, ts, seq):
    m = z_pool.shape[0]
    hb = ts // POOL_HALO
    kern = functools.partial(_pool_kernel, ts=ts, tiles_per_seq=seq // ts)
    return pl.pallas_call(
        kern,
        grid=(m // ts,),
        in_specs=[pl.BlockSpec((ts, POOL_DIM), lambda i: (i, 0)),
                  pl.BlockSpec((POOL_HALO, POOL_DIM), lambda i: (jnp.maximum(i * hb - 1, 0), 0)),
                  _layer_resident(pool_w.shape[1:], l), _resident((1, POOL_DIM))],
        out_specs=pl.BlockSpec((ts, POOL_DIM), lambda i: (i, 0)),
        out_shape=jax.ShapeDtypeStruct((m, POOL_DIM), BF16),
        scratch_shapes=[pltpu.VMEM((ts + POOL_HALO, POOL_DIM), F32)],
        compiler_params=_params(("parallel",), 0),
        name="pool",
    )(z_pool, z_pool, pool_w, pool_scale)


_CONV_ROWS = 64


def _glu(c):
    c = c.astype(F32)
    return c[:, :CONV_DIM] * jax.nn.sigmoid(c[:, CONV_DIM:])


def _conv_kernel(c_ref, halo_ref, w_ref, b_ref, ng_ref, nb_ref, o_ref, ext_ref, y_ref, *, ts, tiles_per_seq):
    i = pl.program_id(0)
    first = (i % tiles_per_seq) == 0
    ext_ref[0, 0:CONV_HALO, :] = jnp.where(first, 0.0, _glu(halo_ref[...]))
    ext_ref[0, CONV_HALO:, :] = _glu(c_ref[...])
    span = ts + CONV_HALO - SUBLANES
    for p in range(1, SUBLANES):
        ext_ref[p, 0:span, :] = ext_ref[0, p:p + span, :]
    base = CONV_HALO - (CONV_WIDTH - 1)
    for r in range(0, ts, _CONV_ROWS):
        for c in range(0, CONV_DIM, LANES):
            acc = None
            for k in range(CONV_WIDTH):
                a, p = divmod(base + k, SUBLANES)
                u = SUBLANES * a + r
                term = ext_ref[p, u:u + _CONV_ROWS, c:c + LANES] * w_ref[k:k + 1, c:c + LANES]
                acc = term if acc is None else acc + term
            y_ref[r:r + _CONV_ROWS, c:c + LANES] = acc
    y = _layernorm(y_ref[...] + b_ref[...], ng_ref[...], nb_ref[...])
    o_ref[...] = (y * jax.nn.sigmoid(y)).astype(BF16)


def _conv(z_conv, conv_w, conv_b, norm_g, norm_b, ts, seq):
    m = z_conv.shape[0]
    hb = ts // CONV_HALO
    kern = functools.partial(_conv_kernel, ts=ts, tiles_per_seq=seq // ts)
    vec = _resident((1, CONV_DIM))
    return pl.pallas_call(
        kern,
        grid=(m // ts,),
        in_specs=[pl.BlockSpec((ts, 2 * CONV_DIM), lambda i: (i, 0)),
                  pl.BlockSpec((CONV_HALO, 2 * CONV_DIM), lambda i: (jnp.maximum(i * hb - 1, 0), 0)),
                  _resident((CONV_WIDTH, CONV_DIM)), vec, vec, vec],
        out_specs=pl.BlockSpec((ts, CONV_DIM), lambda i: (i, 0)),
        out_shape=jax.ShapeDtypeStruct((m, CONV_DIM), BF16),
        scratch_shapes=[pltpu.VMEM((SUBLANES, ts + CONV_HALO, CONV_DIM), F32), pltpu.VMEM((ts, CONV_DIM), F32)],
        compiler_params=_params(("parallel",), (SUBLANES + 2) * (ts + CONV_HALO) * CONV_DIM * 4 + 8 * ts * CONV_DIM * 4),
        name="conv",
    )(z_conv, z_conv, conv_w, conv_b, norm_g, norm_b)


def _gelu_tanh(x):
    return 0.5 * x * (1.0 + jnp.tanh(0.7978845608028654 * (x + 0.044715 * (x * x * x))))


def _sgu_kernel(z_ref, ng_ref, nb_ref, w_ref, bt_ref, o_ref, *, ts):
    z = _gelu_tanh(z_ref[...].astype(F32))
    u = z[:, :SGU_DIM]
    v = _layernorm(z[:, SGU_DIM:], ng_ref[...], nb_ref[...]).astype(BF16)
    row = lax.broadcasted_iota(jnp.int32, (CHUNK, CHUNK), 0)
    col = lax.broadcasted_iota(jnp.int32, (CHUNK, CHUNK), 1)
    n_chunks = ts // CHUNK
    for g in range(SGU_GROUPS):
        c0, c1 = g * SGU_GDIM, (g + 1) * SGU_GDIM
        w = jnp.where(col <= row, w_ref[g], 0.0).astype(BF16)
        rhs = jnp.concatenate([v[n * CHUNK:(n + 1) * CHUNK, c0:c1] for n in range(n_chunks)], axis=1)
        sp = _dot(w, rhs) + bt_ref[:, g:g + 1]
        for n in range(n_chunks):
            o_ref[n * CHUNK:(n + 1) * CHUNK, c0:c1] = (
                u[n * CHUNK:(n + 1) * CHUNK, c0:c1] * sp[:, n * SGU_GDIM:(n + 1) * SGU_GDIM]).astype(BF16)


def _sgu(z_sgu, norm_g, norm_b, sgu_w, sgu_bt, ts):
    m = z_sgu.shape[0]
    vec = _resident((1, SGU_DIM))
    return pl.pallas_call(
        functools.partial(_sgu_kernel, ts=ts),
        grid=(m // ts,),
        in_specs=[pl.BlockSpec((ts, 2 * SGU_DIM), lambda i: (i, 0)), vec, vec,
                  _resident(sgu_w.shape), _resident(sgu_bt.shape)],
        out_specs=pl.BlockSpec((ts, SGU_DIM), lambda i: (i, 0)),
        out_shape=jax.ShapeDtypeStruct((m, SGU_DIM), BF16),
        compiler_params=_params(("parallel",), 0),
        name="sgu",
    )(z_sgu, norm_g, norm_b, sgu_w, sgu_bt)


def _mla_prep_kernel(cq_ref, ckv_ref, kr_ref, c_ref, s1_ref, s2_ref, qg_ref, kvg_ref, wq_ref, wkv_ref,
                     q_ref, kn_ref, kro_ref, v_ref):
    c, s1, s2 = c_ref[...], s1_ref[...], s2_ref[...]
    scale = (QK_NOPE + QK_ROPE) ** -0.5 * LOG2_E
    qn = _rms(cq_ref[...].astype(F32), qg_ref[...]).astype(BF16)
    for h in range(MLA_HEADS):
        o = h * Q_HEAD_PAD
        qh = _dot(qn, wq_ref[:, o:o + Q_HEAD_PAD]) * scale
        q_ref[:, o:o + LANES] = qh[:, :LANES].astype(BF16)
        q_ref[:, o + LANES:o + Q_HEAD_PAD] = _rope(qh[:, LANES:], c, s1, s2).astype(BF16)
    kvn = _rms(ckv_ref[...].astype(F32), kvg_ref[...]).astype(BF16)
    width = MLA_HEADS * QK_NOPE
    for o in range(0, width, 512):
        kn_ref[:, o:o + 512] = _dot(kvn, wkv_ref[:, o:o + 512]).astype(BF16)
        v_ref[:, o:o + 512] = _dot(kvn, wkv_ref[:, width + o:width + o + 512]).astype(BF16)
    kro_ref[...] = _rope(kr_ref[...].astype(F32), c, s1, s2).astype(BF16)


def _mla_prep(cq, ckv, kr, tables, q_norm_g, kv_norm_g, wq, wkv, l, ts):
    m = cq.shape[0]
    row = lambda i: (i, 0)
    tab = pl.BlockSpec((ts, LANES), row)
    hq = MLA_HEADS * Q_HEAD_PAD
    hk = MLA_HEADS * QK_NOPE
    return pl.pallas_call(
        _mla_prep_kernel,
        grid=(m // ts,),
        in_specs=[pl.BlockSpec((ts, Q_LORA), row), pl.BlockSpec((ts, KV_LORA), row),
                  pl.BlockSpec((ts, KR_PAD), row), tab, tab, tab,
                  _resident((1, Q_LORA)), _resident((1, KV_LORA)),
                  _layer_resident(wq.shape[1:], l), _layer_resident(wkv.shape[1:], l)],
        out_specs=[pl.BlockSpec((ts, hq), row), pl.BlockSpec((ts, hk), row),
                   pl.BlockSpec((ts, KR_PAD), row), pl.BlockSpec((ts, hk), row)],
        out_shape=[jax.ShapeDtypeStruct((m, hq), BF16), jax.ShapeDtypeStruct((m, hk), BF16),
                   jax.ShapeDtypeStruct((m, KR_PAD), BF16), jax.ShapeDtypeStruct((m, hk), BF16)],
        compiler_params=_params(("parallel",), 32 * 1024 * 1024),
        name="mla_prep",
    )(cq, ckv, kr, *tables, q_norm_g, kv_norm_g, wq, wkv)


_ATTN_ROWS = 64


def _attn_kernel(q_ref, kn_ref, kr_ref, v_ref, o_ref, s_ref, p_ref, m_ref, l_ref, acc_ref, *, tq, tk):
    qi = pl.program_id(2)
    n_diag = tq // tk
    m_ref[...] = jnp.full(m_ref.shape, NEG_BIG, F32)
    l_ref[...] = jnp.zeros(l_ref.shape, F32)
    acc_ref[...] = jnp.zeros(acc_ref.shape, F32)

    def block(j, row_start, mask_shift):
        k0 = pl.multiple_of(j * tk, tk)
        k = jnp.concatenate([kn_ref[pl.ds(k0, tk), :], kr_ref[pl.ds(k0, tk), :]], axis=-1)
        s_ref[row_start:, :] = lax.dot_general(q_ref[row_start:, :], k, (((1,), (1,)), ((), ())),
                                               preferred_element_type=F32)
        for c in range(row_start, tq, _ATTN_ROWS):
            rows = slice(c, c + _ATTN_ROWS)
            s = s_ref[rows, :]
            if mask_shift is not None and mask_shift + tk - 1 > c:
                row = lax.broadcasted_iota(jnp.int32, s.shape, 0) + c
                col = lax.broadcasted_iota(jnp.int32, s.shape, 1) + mask_shift
                s = jnp.where(col <= row, s, NEG_BIG)
            m_prev = m_ref[rows, :]
            m_next = jnp.maximum(m_prev, jnp.max(s, axis=1, keepdims=True))
            alpha = jnp.exp2(m_prev - m_next)
            p = jnp.exp2(s - jnp.tile(m_next, (1, tk // LANES)))
            l_ref[rows, :] = alpha * l_ref[rows, :] + jnp.sum(p, axis=1, keepdims=True)
            m_ref[rows, :] = m_next
            acc_ref[rows, :] = alpha * acc_ref[rows, :]
            p_ref[rows, :] = p.astype(BF16)
        acc_ref[row_start:, :] += _dot(p_ref[row_start:, :], v_ref[pl.ds(k0, tk), :])

    def body(j, carry):
        block(j, 0, None)
        return carry

    lax.fori_loop(0, qi * n_diag, body, 0)
    for d in range(n_diag):
        block(qi * n_diag + d, d * tk, d * tk)
    o_ref[...] = (acc_ref[...] / l_ref[...]).astype(BF16)


def _attention(q, kn, kr, v, batch, seq, tq, tk):
    m = q.shape[0]
    nq = seq // tq
    kern = functools.partial(_attn_kernel, tq=tq, tk=tk)
    vmem = (2 * 2 * (tq * (Q_HEAD_PAD + V_DIM) + seq * (QK_NOPE + KR_PAD + V_DIM))
            + tq * tk * (4 + 2) + tq * (2 * LANES + V_DIM) * 4 + 12 * 1024 * 1024)
    return pl.pallas_call(
        kern,
        grid=(batch, MLA_HEADS, nq),
        in_specs=[pl.BlockSpec((tq, Q_HEAD_PAD), lambda b, h, i: (b * nq + i, h)),
                  pl.BlockSpec((seq, QK_NOPE), lambda b, h, i: (b, h)),
                  pl.BlockSpec((seq, KR_PAD), lambda b, h, i: (b, 0)),
                  pl.BlockSpec((seq, V_DIM), lambda b, h, i: (b, h))],
        out_specs=pl.BlockSpec((tq, V_DIM), lambda b, h, i: (b * nq + i, h)),
        out_shape=jax.ShapeDtypeStruct((m, MLA_HEADS * V_DIM), BF16),
        scratch_shapes=[pltpu.VMEM((tq, tk), F32), pltpu.VMEM((tq, tk), BF16), pltpu.VMEM((tq, LANES), F32),
                        pltpu.VMEM((tq, LANES), F32), pltpu.VMEM((tq, V_DIM), F32)],
        compiler_params=_params(("parallel", "parallel", "arbitrary"), vmem),
        name="attention",
    )(q, kn, kr, v)


def _merge_kernel(h_ref, bp_ref, bc_ref, bs_ref, ba_ref, g0_ref, g1_ref, g2_ref, g3_ref,
                  pp_ref, pc_ref, ps_ref, pa_ref, o_ref):
    h = h_ref[...]
    acc = None
    for b_ref, g_ref, p_ref in ((bp_ref, g0_ref, pp_ref), (bc_ref, g1_ref, pc_ref),
                                (bs_ref, g2_ref, ps_ref), (ba_ref, g3_ref, pa_ref)):
        term = jax.nn.sigmoid(_dot(h, g_ref[...])) * _dot(b_ref[...], p_ref[...])
        acc = term if acc is None else acc + term
    o_ref[...] = acc.astype(BF16)


def _merge(h, branches, w_gate, projs, l, tm, tn):
    m = h.shape[0]
    nj = D_MODEL // tn
    row = lambda i, j: (i, 0)
    in_specs = [pl.BlockSpec((tm, D_MODEL), row)]
    in_specs += [pl.BlockSpec((tm, b.shape[1]), row) for b in branches]
    in_specs += [pl.BlockSpec((None, D_MODEL, tn), functools.partial(lambda i, j, b: (l, 0, b * nj + j), b=b))
                 for b in range(N_BRANCH)]
    in_specs += [pl.BlockSpec((None, p.shape[1], tn), lambda i, j: (l, 0, j)) for p in projs]
    kin = sum(b.shape[1] for b in branches)
    vmem = 2 * 2 * (tm * D_MODEL + tm * kin + N_BRANCH * D_MODEL * tn + kin * tn + tm * tn) + 6 * tm * tn * 4
    return pl.pallas_call(
        _merge_kernel,
        grid=(m // tm, nj),
        in_specs=in_specs,
        out_specs=pl.BlockSpec((tm, tn), lambda i, j: (i, j)),
        out_shape=jax.ShapeDtypeStruct((m, D_MODEL), BF16),
        compiler_params=_params(("parallel", "arbitrary"), vmem),
        name="merge",
    )(h, *branches, w_gate, w_gate, w_gate, w_gate, *projs)


def _outproj_kernel(m_ref, w_ref, x_ref, gpost_ref, gpre_ref, x1_ref, h2_ref, y_ref):
    mg = m_ref[...]
    for c in range(0, D_MODEL, 512):
        y_ref[:, c:c + 512] = _dot(mg, w_ref[:, c:c + 512])
    x1 = x_ref[...] + _rms(y_ref[...], gpost_ref[...])
    x1_ref[...] = x1
    h2_ref[...] = _rms(x1, gpre_ref[...]).astype(BF16)


def _outproj(merged, w_out, l, x, g_post, g_pre, tm):
    m = x.shape[0]
    row = lambda i: (i, 0)
    vec = _resident((1, D_MODEL))
    vmem = 2 * tm * D_MODEL * (2 + 4 + 4 + 2) + D_MODEL * D_MODEL * 2 + 4 * tm * D_MODEL * 4
    return pl.pallas_call(
        _outproj_kernel,
        grid=(m // tm,),
        in_specs=[pl.BlockSpec((tm, D_MODEL), row), _layer_resident((D_MODEL, D_MODEL), l),
                  pl.BlockSpec((tm, D_MODEL), row), vec, vec],
        out_specs=[pl.BlockSpec((tm, D_MODEL), row), pl.BlockSpec((tm, D_MODEL), row)],
        out_shape=[jax.ShapeDtypeStruct((m, D_MODEL), F32), jax.ShapeDtypeStruct((m, D_MODEL), BF16)],
        scratch_shapes=[pltpu.VMEM((tm, D_MODEL), F32)],
        compiler_params=_params(("parallel",), vmem),
        name="outproj",
    )(merged, w_out, x, g_post, g_pre)


_MLP_COLS = 512


def _mlp_kernel(h_ref, wu_ref, wd_ref, xs_ref, g_ref, o_ref, a_ref, x_ref):
    f = pl.program_id(1)
    tf = a_ref.shape[1]
    rows = xs_ref.shape[0]
    x_ref[pl.ds(pl.multiple_of(f * rows, rows), rows), :] = xs_ref[...]

    @pl.when(f == 0)
    def _():
        o_ref[...] = jnp.zeros_like(o_ref)

    h = h_ref[...]
    for c in range(0, tf, _MLP_COLS):
        up = _dot(h, wu_ref[:, c:c + _MLP_COLS])
        a_ref[:, c:c + _MLP_COLS] = jnp.square(jnp.maximum(up, 0.0)).astype(BF16)
    a = a_ref[...]
    for c in range(0, D_MODEL, _MLP_COLS):
        o_ref[:, c:c + _MLP_COLS] += _dot(a, wd_ref[:, c:c + _MLP_COLS])

    @pl.when(f == pl.num_programs(1) - 1)
    def _():
        o_ref[...] = x_ref[...] + _rms(o_ref[...], g_ref[...])


def _mlp(h2, w_up, w_down, l, x1, g_post, tm, tf):
    m = x1.shape[0]
    row = lambda i, f: (i, 0)
    nf = D_FF // tf
    rows = tm // nf
    assert tm % nf == 0 and rows % SUBLANES == 0
    vmem = (tm * D_MODEL * (2 * 2 + 4 + 2 * 4) + 2 * 2 * 2 * D_MODEL * tf + tm * tf * 2
            + 3 * tm * _MLP_COLS * 4 + 4 * 1024 * 1024)
    return pl.pallas_call(
        _mlp_kernel,
        grid=(m // tm, nf),
        in_specs=[pl.BlockSpec((tm, D_MODEL), row), pl.BlockSpec((None, D_MODEL, tf), lambda i, f: (l, 0, f)),
                  pl.BlockSpec((None, tf, D_MODEL), lambda i, f: (l, f, 0)),
                  pl.BlockSpec((rows, D_MODEL), lambda i, f: (i * nf + f, 0)),
                  _resident((1, D_MODEL))],
        out_specs=pl.BlockSpec((tm, D_MODEL), row),
        out_shape=jax.ShapeDtypeStruct((m, D_MODEL), F32),
        scratch_shapes=[pltpu.VMEM((tm, tf), BF16), pltpu.VMEM((tm, D_MODEL), F32)],
        compiler_params=_params(("parallel", "arbitrary"), vmem),
        name="mlp",
    )(h2, w_up, w_down, x1, g_post)


def _small_in_weight(w_in):
    depth = w_in.shape[0]
    pad = jnp.zeros((depth, D_MODEL, KR_PAD - QK_ROPE), BF16)
    return jnp.concatenate([w_in[:, :, :OFF_GATE].astype(BF16), pad], axis=2)


def _q_weight(w_uq):
    depth = w_uq.shape[0]
    w = w_uq.astype(BF16).reshape(depth, Q_LORA, MLA_HEADS, QK_NOPE + QK_ROPE)
    pad = jnp.zeros((depth, Q_LORA, MLA_HEADS, Q_HEAD_PAD - QK_NOPE - QK_ROPE), BF16)
    return jnp.concatenate([w, pad], axis=3).reshape(depth, Q_LORA, MLA_HEADS * Q_HEAD_PAD)


def _kv_weight(w_ukv):
    depth = w_ukv.shape[0]
    w = w_ukv.astype(BF16).reshape(depth, KV_LORA, MLA_HEADS, QK_NOPE + V_DIM)
    k = w[..., :QK_NOPE].reshape(depth, KV_LORA, MLA_HEADS * QK_NOPE)
    v = w[..., QK_NOPE:].reshape(depth, KV_LORA, MLA_HEADS * V_DIM)
    return jnp.concatenate([k, v], axis=2)


def _tile(n, want):
    t = min(n, want)
    assert n % t == 0, (n, t)
    return t


def kernel(x, positions, pre_mix_g, w_in, pool_w, pool_scale, pool_proj, conv_w, conv_b, conv_norm_g, conv_norm_b, conv_proj, sgu_norm_g, sgu_norm_b, sgu_w, sgu_b, sgu_proj, q_norm_g, w_uq, kv_norm_g, w_ukv, attn_proj, w_out, post_mix_g, pre_mlp_g, w_up, w_down, post_mlp_g):
    batch, seq, d = x.shape
    assert d == D_MODEL and seq % CHUNK == 0
    m = batch * seq
    depth = w_in.shape[0]
    ts = _tile(seq, 512)
    ts_conv = _tile(seq, 256)
    tq = _tile(seq, 4096)
    tk = _tile(tq, 512)
    tm = _tile(m, 512)
    tm_mlp = _tile(m, 1024)

    xf = x.reshape(m, D_MODEL)
    tables = _rope_tables(positions.reshape(m, 1), ts)
    row = lambda v: v[None, :]

    w_small = _small_in_weight(w_in)
    w_gate = w_in[:, :, OFF_GATE:].astype(BF16)
    pool_wb = pool_w.astype(BF16)
    projs = tuple(p.astype(BF16) for p in (pool_proj, conv_proj, sgu_proj, attn_proj))
    wq, wkv = _q_weight(w_uq), _kv_weight(w_ukv)
    w_outb, w_upb, w_downb = w_out.astype(BF16), w_up.astype(BF16), w_down.astype(BF16)

    for l in range(depth):
        h, z_pool, z_conv, z_sgu, cq, ckv, kr = _inproj(xf, row(pre_mix_g[l]), w_small, l, tm)
        y_pool = _pool(z_pool, pool_wb, l, row(pool_scale[l]), ts, seq)
        y_conv = _conv(z_conv, conv_w[l], row(conv_b[l]), row(conv_norm_g[l]), row(conv_norm_b[l]), ts_conv, seq)
        y_sgu = _sgu(z_sgu, row(sgu_norm_g[l]), row(sgu_norm_b[l]), sgu_w[l], sgu_b[l].T, ts)
        q, kn, kro, v = _mla_prep(cq, ckv, kr, tables, row(q_norm_g[l]), row(kv_norm_g[l]), wq, wkv, l, ts)
        y_attn = _attention(q, kn, kro, v, batch, seq, tq, tk)
        merged = _merge(h, (y_pool, y_conv, y_sgu, y_attn), w_gate, projs, l, tm_mlp, 512)
        x1, h2 = _outproj(merged, w_outb, l, xf, row(post_mix_g[l]), row(pre_mlp_g[l]), tm)
        xf = _mlp(h2, w_upb, w_downb, l, x1, row(post_mlp_g[l]), tm_mlp, 512)
    return xf.reshape(batch, seq, D_MODEL)
```

```python
import functools

import jax
import jax.numpy as jnp
from jax import lax
from jax.experimental import pallas as pl
from jax.experimental.pallas import tpu as pltpu

F32 = jnp.float32
BF16 = jnp.bfloat16

D_MODEL = 2048
EPS = 1e-6
N_BRANCH = 4
D_FF = 4 * D_MODEL
POOL_DIM = 512
POOL_WINDOWS = (2, 4, 8, 16)
POOL_GDIM = POOL_DIM // len(POOL_WINDOWS)
CONV_DIM = 512
CONV_WIDTH = 31
SGU_DIM = 512
SGU_GROUPS = 4
SGU_GDIM = SGU_DIM // SGU_GROUPS
CHUNK = 128
MLA_HEADS = 8
Q_LORA = 512
KV_LORA = 512
QK_NOPE = 128
QK_ROPE = 64
V_DIM = 128
ROPE_THETA = 10000.0
OFF_KR = POOL_DIM + 2 * CONV_DIM + 2 * SGU_DIM + Q_LORA + KV_LORA
OFF_GATE = OFF_KR + QK_ROPE

LANES = 128
SUBLANES = 8
Q_HEAD_PAD = 2 * LANES
KR_PAD = LANES
N_SMALL = OFF_KR + KR_PAD
POOL_HALO = 16
CONV_HALO = 32
NEG_BIG = -1e30
LOG2_E = 1.4426950408889634
VMEM_CAP = 56 * 1024 * 1024

_SEGS = (POOL_DIM, 2 * CONV_DIM, 2 * SGU_DIM, Q_LORA, KV_LORA, KR_PAD)


def _params(semantics, vmem_bytes):
    return pltpu.CompilerParams(dimension_semantics=semantics,
                                vmem_limit_bytes=int(min(VMEM_CAP, max(vmem_bytes, 16 * 1024 * 1024))))


def _resident(shape):
    nd = len(shape)
    return pl.BlockSpec(shape, lambda *_: (0,) * nd, pipeline_mode=pl.Buffered(1))


def _layer_resident(shape, l):
    nd = len(shape)
    return pl.BlockSpec((None,) + tuple(shape), lambda *_: (l,) + (0,) * nd, pipeline_mode=pl.Buffered(1))


def _rms(x, g):
    return x * lax.rsqrt(jnp.mean(x * x, axis=-1, keepdims=True) + EPS) * g


def _layernorm(x, g, b):
    mu = jnp.mean(x, axis=-1, keepdims=True)
    xc = x - mu
    var = jnp.mean(xc * xc, axis=-1, keepdims=True)
    return xc * lax.rsqrt(var + EPS) * g + b


def _dot(a, b):
    return jnp.dot(a, b, preferred_element_type=F32)


def _rope_table_kernel(pos_ref, freq_ref, c_ref, s1_ref, s2_ref):
    ang = pos_ref[...].astype(F32) * freq_ref[...]
    lane = lax.broadcasted_iota(jnp.int32, ang.shape, 1)
    cos = jnp.cos(ang)
    sin = jnp.sin(ang)
    half = QK_ROPE // 2
    c_ref[...] = jnp.where(lane < QK_ROPE, cos, 0.0)
    s1_ref[...] = jnp.where(lane < half, 0.0, jnp.where(lane < QK_ROPE, sin, 0.0))
    s2_ref[...] = jnp.where(lane < half, -sin, 0.0)


def _rope_tables(pos, tm):
    m = pos.shape[0]
    inv_freq = ROPE_THETA ** (-jnp.arange(0, QK_ROPE, 2, dtype=F32) / QK_ROPE)
    freq = jnp.concatenate([inv_freq, inv_freq, jnp.zeros((LANES - QK_ROPE,), F32)])[None, :]
    out = jax.ShapeDtypeStruct((m, LANES), F32)
    spec = pl.BlockSpec((tm, LANES), lambda i: (i, 0))
    return pl.pallas_call(
        _rope_table_kernel,
        grid=(m // tm,),
        in_specs=[pl.BlockSpec((tm, 1), lambda i: (i, 0)), pl.BlockSpec((1, LANES), lambda i: (0, 0))],
        out_specs=[spec, spec, spec],
        out_shape=[out, out, out],
        compiler_params=_params(("parallel",), 0),
        name="rope_tables",
    )(pos, freq)


def _rope(chunk, c, s1, s2):
    half = QK_ROPE // 2
    return (chunk * c + pltpu.roll(chunk, half, 1) * s1
            + pltpu.roll(chunk, LANES - half, 1) * s2)


def _inproj_kernel(x_ref, g_ref, w_ref, h_ref, *z_refs):
    h = _rms(x_ref[...], g_ref[...]).astype(BF16)
    h_ref[...] = h
    off = 0
    for z_ref, width in zip(z_refs, _SEGS):
        for c in range(0, width, 512):
            cw = min(512, width - c)
            z_ref[:, c:c + cw] = _dot(h, w_ref[:, off + c:off + c + cw]).astype(BF16)
        off += width


def _inproj(x, g, w_small, l, tm):
    m = x.shape[0]
    row = lambda i: (i, 0)
    out_shape = [jax.ShapeDtypeStruct((m, D_MODEL), BF16)]
    out_specs = [pl.BlockSpec((tm, D_MODEL), row)]
    for width in _SEGS:
        out_shape.append(jax.ShapeDtypeStruct((m, width), BF16))
        out_specs.append(pl.BlockSpec((tm, width), row))
    vmem = (2 * tm * D_MODEL * 4 + 2 * tm * D_MODEL * 2 + D_MODEL * N_SMALL * 2
            + 2 * tm * N_SMALL * 2 + 4 * tm * D_MODEL * 4)
    return pl.pallas_call(
        _inproj_kernel,
        grid=(m // tm,),
        in_specs=[pl.BlockSpec((tm, D_MODEL), row), _resident((1, D_MODEL)),
                  _layer_resident((D_MODEL, N_SMALL), l)],
        out_specs=out_specs,
        out_shape=out_shape,
        compiler_params=_params(("parallel",), vmem),
        name="inproj",
    )(x, g, w_small)


def _pool_kernel(a_ref, halo_ref, w_ref, scale_ref, o_ref, ext_ref, *, ts, tiles_per_seq):
    i = pl.program_id(0)
    first = (i % tiles_per_seq) == 0
    a = a_ref[...].astype(F32)
    ext_ref[0:POOL_HALO, :] = jnp.where(first, 0.0, halo_ref[...].astype(F32))
    ext_ref[POOL_HALO:, :] = a
    t = (i % tiles_per_seq) * ts + lax.broadcasted_iota(jnp.int32, (ts, 1), 0)
    for gi, w in enumerate(POOL_WINDOWS):
        c0, c1 = gi * POOL_GDIM, (gi + 1) * POOL_GDIM
        win = ext_ref[POOL_HALO:POOL_HALO + ts, c0:c1]
        for d in range(1, w):
            win = win + ext_ref[POOL_HALO - d:POOL_HALO - d + ts, c0:c1]
        count = jnp.minimum(t + 1, w).astype(F32)
        pooled = (win / count - a[:, c0:c1]).astype(BF16)
        o_ref[:, c0:c1] = (_dot(pooled, w_ref[gi]) * scale_ref[:, c0:c1]).astype(BF16)


def _pool(z_pool, pool_w, l, pool_scale, ts, seq):
    m = z_pool.shape[0]
    hb = ts // POOL_HALO
    kern = functools.partial(_pool_kernel, ts=ts, tiles_per_seq=seq // ts)
    return pl.pallas_call(
        kern,
        grid=(m // ts,),
        in_specs=[pl.BlockSpec((ts, POOL_DIM), lambda i: (i, 0)),
                  pl.BlockSpec((POOL_HALO, POOL_DIM), lambda i: (jnp.maximum(i * hb - 1, 0), 0)),
                  _layer_resident(pool_w.shape[1:], l), _resident((1, POOL_DIM))],
        out_specs=pl.BlockSpec((ts, POOL_DIM), lambda i: (i, 0)),
        out_shape=jax.ShapeDtypeStruct((m, POOL_DIM), BF16),
        scratch_shapes=[pltpu.VMEM((ts + POOL_HALO, POOL_DIM), F32)],
        compiler_params=_params(("parallel",), 0),
        name="pool",
    )(z_pool, z_pool, pool_w, pool_scale)


_CONV_ROWS = 64


def _glu(c):
    c = c.astype(F32)
    return c[:, :CONV_DIM] * jax.nn.sigmoid(c[:, CONV_DIM:])


def _conv_kernel(c_ref, halo_ref, w_ref, b_ref, ng_ref, nb_ref, o_ref, ext_ref, y_ref, *, ts, tiles_per_seq):
    i = pl.program_id(0)
    first = (i % tiles_per_seq) == 0
    ext_ref[0, 0:CONV_HALO, :] = jnp.where(first, 0.0, _glu(halo_ref[...]))
    ext_ref[0, CONV_HALO:, :] = _glu(c_ref[...])
    span = ts + CONV_HALO - SUBLANES
    for p in range(1, SUBLANES):
        ext_ref[p, 0:span, :] = ext_ref[0, p:p + span, :]
    base = CONV_HALO - (CONV_WIDTH - 1)
    for r in range(0, ts, _CONV_ROWS):
        for c in range(0, CONV_DIM, LANES):
            acc = None
            for k in range(CONV_WIDTH):
                a, p = divmod(base + k, SUBLANES)
                u = SUBLANES * a + r
                term = ext_ref[p, u:u + _CONV_ROWS, c:c + LANES] * w_ref[k:k + 1, c:c + LANES]
                acc = term if acc is None else acc + term
            y_ref[r:r + _CONV_ROWS, c:c + LANES] = acc
    y = _layernorm(y_ref[...] + b_ref[...], ng_ref[...], nb_ref[...])
    o_ref[...] = (y * jax.nn.sigmoid(y)).astype(BF16)


def _conv(z_conv, conv_w, conv_b, norm_g, norm_b, ts, seq):
    m = z_conv.shape[0]
    hb = ts // CONV_HALO
    kern = functools.partial(_conv_kernel, ts=ts, tiles_per_seq=seq // ts)
    vec = _resident((1, CONV_DIM))
    return pl.pallas_call(
        kern,
        grid=(m // ts,),
        in_specs=[pl.BlockSpec((ts, 2 * CONV_DIM), lambda i: (i, 0)),
                  pl.BlockSpec((CONV_HALO, 2 * CONV_DIM), lambda i: (jnp.maximum(i * hb - 1, 0), 0)),
                  _resident((CONV_WIDTH, CONV_DIM)), vec, vec, vec],
        out_specs=pl.BlockSpec((ts, CONV_DIM), lambda i: (i, 0)),
        out_shape=jax.ShapeDtypeStruct((m, CONV_DIM), BF16),
        scratch_shapes=[pltpu.VMEM((SUBLANES, ts + CONV_HALO, CONV_DIM), F32), pltpu.VMEM((ts, CONV_DIM), F32)],
        compiler_params=_params(("parallel",), (SUBLANES + 2) * (ts + CONV_HALO) * CONV_DIM * 4 + 8 * ts * CONV_DIM * 4),
        name="conv",
    )(z_conv, z_conv, conv_w, conv_b, norm_g, norm_b)


def _gelu_tanh(x):
    return 0.5 * x * (1.0 + jnp.tanh(0.7978845608028654 * (x + 0.044715 * (x * x * x))))


def _sgu_kernel(z_ref, ng_ref, nb_ref, w_ref, bt_ref, o_ref, *, ts):
    z = _gelu_tanh(z_ref[...].astype(F32))
    u = z[:, :SGU_DIM]
    v = _layernorm(z[:, SGU_DIM:], ng_ref[...], nb_ref[...]).astype(BF16)
    row = lax.broadcasted_iota(jnp.int32, (CHUNK, CHUNK), 0)
    col = lax.broadcasted_iota(jnp.int32, (CHUNK, CHUNK), 1)
    n_chunks = ts // CHUNK
    for g in range(SGU_GROUPS):
        c0, c1 = g * SGU_GDIM, (g + 1) * SGU_GDIM
        w = jnp.where(col <= row, w_ref[g], 0.0).astype(BF16)
        rhs = jnp.concatenate([v[n * CHUNK:(n + 1) * CHUNK, c0:c1] for n in range(n_chunks)], axis=1)
        sp = _dot(w, rhs) + bt_ref[:, g:g + 1]
        for n in range(n_chunks):
            o_ref[n * CHUNK:(n + 1) * CHUNK, c0:c1] = (
                u[n * CHUNK:(n + 1) * CHUNK, c0:c1] * sp[:, n * SGU_GDIM:(n + 1) * SGU_GDIM]).astype(BF16)


def _sgu(z_sgu, norm_g, norm_b, sgu_w, sgu_bt, ts):
    m = z_sgu.shape[0]
    vec = _resident((1, SGU_DIM))
    return pl.pallas_call(
        functools.partial(_sgu_kernel, ts=ts),
        grid=(m // ts,),
        in_specs=[pl.BlockSpec((ts, 2 * SGU_DIM), lambda i: (i, 0)), vec, vec,
                  _resident(sgu_w.shape), _resident(sgu_bt.shape)],
        out_specs=pl.BlockSpec((ts, SGU_DIM), lambda i: (i, 0)),
        out_shape=jax.ShapeDtypeStruct((m, SGU_DIM), BF16),
        compiler_params=_params(("parallel",), 0),
        name="sgu",
    )(z_sgu, norm_g, norm_b, sgu_w, sgu_bt)


def _mla_prep_kernel(cq_ref, ckv_ref, kr_ref, c_ref, s1_ref, s2_ref, qg_ref, kvg_ref, wq_ref, wkv_ref,
                     q_ref, kn_ref, kro_ref, v_ref):
    c, s1, s2 = c_ref[...], s1_ref[...], s2_ref[...]
    scale = (QK_NOPE + QK_ROPE) ** -0.5 * LOG2_E
    qn = _rms(cq_ref[...].astype(F32), qg_ref[...]).astype(BF16)
    for h in range(MLA_HEADS):
        o = h * Q_HEAD_PAD
        qh = _dot(qn, wq_ref[:, o:o + Q_HEAD_PAD]) * scale
        q_ref[:, o:o + LANES] = qh[:, :LANES].astype(BF16)
        q_ref[:, o + LANES:o + Q_HEAD_PAD] = _rope(qh[:, LANES:], c, s1, s2).astype(BF16)
    kvn = _rms(ckv_ref[...].astype(F32), kvg_ref[...]).astype(BF16)
    width = MLA_HEADS * QK_NOPE
    for o in range(0, width, 512):
        kn_ref[:, o:o + 512] = _dot(kvn, wkv_ref[:, o:o + 512]).astype(BF16)
        v_ref[:, o:o + 512] = _dot(kvn, wkv_ref[:, width + o:width + o + 512]).astype(BF16)
    kro_ref[...] = _rope(kr_ref[...].astype(F32), c, s1, s2).astype(BF16)


def _mla_prep(cq, ckv, kr, tables, q_norm_g, kv_norm_g, wq, wkv, l, ts):
    m = cq.shape[0]
    row = lambda i: (i, 0)
    tab = pl.BlockSpec((ts, LANES), row)
    hq = MLA_HEADS * Q_HEAD_PAD
    hk = MLA_HEADS * QK_NOPE
    return pl.pallas_call(
        _mla_prep_kernel,
        grid=(m // ts,),
        in_specs=[pl.BlockSpec((ts, Q_LORA), row), pl.BlockSpec((ts, KV_LORA), row),
                  pl.BlockSpec((ts, KR_PAD), row), tab, tab, tab,
                  _resident((1, Q_LORA)), _resident((1, KV_LORA)),
                  _layer_resident(wq.shape[1:], l), _layer_resident(wkv.shape[1:], l)],
        out_specs=[pl.BlockSpec((ts, hq), row), pl.BlockSpec((ts, hk), row),
                   pl.BlockSpec((ts, KR_PAD), row), pl.BlockSpec((ts, hk), row)],
        out_shape=[jax.ShapeDtypeStruct((m, hq), BF16), jax.ShapeDtypeStruct((m, hk), BF16),
                   jax.ShapeDtypeStruct((m, KR_PAD), BF16), jax.ShapeDtypeStruct((m, hk), BF16)],
        compiler_params=_params(("parallel",), 32 * 1024 * 1024),
        name="mla_prep",
    )(cq, ckv, kr, *tables, q_norm_g, kv_norm_g, wq, wkv)


_ATTN_ROWS = 64


def _attn_kernel(q_ref, kn_ref, kr_ref, v_ref, o_ref, s_ref, p_ref, m_ref, acc_ref, *, tq, tk):
    qi = pl.program_id(2)
    n_diag = tq // tk
    m_ref[...] = jnp.full(m_ref.shape, NEG_BIG, F32)
    acc_ref[...] = jnp.zeros(acc_ref.shape, F32)
    ones = jnp.ones((tk, LANES), BF16)

    def block(j, row_start, mask_shift):
        k0 = pl.multiple_of(j * tk, tk)
        k = jnp.concatenate([kn_ref[pl.ds(k0, tk), :], kr_ref[pl.ds(k0, tk), :]], axis=-1)
        v_ones = jnp.concatenate([v_ref[pl.ds(k0, tk), :], ones], axis=-1)
        s_ref[row_start:, :] = lax.dot_general(q_ref[row_start:, :], k, (((1,), (1,)), ((), ())),
                                               preferred_element_type=F32)
        for c in range(row_start, tq, _ATTN_ROWS):
            rows = slice(c, c + _ATTN_ROWS)
            s = s_ref[rows, :]
            if mask_shift is not None and mask_shift + tk - 1 > c:
                row = lax.broadcasted_iota(jnp.int32, s.shape, 0) + c
                col = lax.broadcasted_iota(jnp.int32, s.shape, 1) + mask_shift
                s = jnp.where(col <= row, s, NEG_BIG)
            m_prev = m_ref[rows, :]
            m_next = jnp.maximum(m_prev, jnp.max(s, axis=1, keepdims=True))
            alpha = jnp.exp2(m_prev - m_next)
            p_ref[rows, :] = jnp.exp2((s - jnp.tile(m_next, (1, tk // LANES))).astype(BF16))
            m_ref[rows, :] = m_next
            acc_ref[rows, :] = jnp.tile(alpha, (1, 2)) * acc_ref[rows, :]
        acc_ref[row_start:, :] += _dot(p_ref[row_start:, :], v_ones)

    def body(j, carry):
        block(j, 0, None)
        return carry

    lax.fori_loop(0, qi * n_diag, body, 0)
    for d in range(n_diag):
        block(qi * n_diag + d, d * tk, d * tk)
    o_ref[...] = (acc_ref[:, :V_DIM] / acc_ref[:, V_DIM:]).astype(BF16)


def _attention(q, kn, kr, v, batch, seq, tq, tk):
    m = q.shape[0]
    nq = seq // tq
    kern = functools.partial(_attn_kernel, tq=tq, tk=tk)
    vmem = (2 * 2 * (tq * (Q_HEAD_PAD + V_DIM) + seq * (QK_NOPE + KR_PAD + V_DIM))
            + tq * tk * (4 + 2) + tq * (2 * LANES + V_DIM) * 4 + 12 * 1024 * 1024)
    return pl.pallas_call(
        kern,
        grid=(batch, MLA_HEADS, nq),
        in_specs=[pl.BlockSpec((tq, Q_HEAD_PAD), lambda b, h, i: (b * nq + i, h)),
                  pl.BlockSpec((seq, QK_NOPE), lambda b, h, i: (b, h)),
                  pl.BlockSpec((seq, KR_PAD), lambda b, h, i: (b, 0)),
                  pl.BlockSpec((seq, V_DIM), lambda b, h, i: (b, h))],
        out_specs=pl.BlockSpec((tq, V_DIM), lambda b, h, i: (b * nq + i, h)),
        out_shape=jax.ShapeDtypeStruct((m, MLA_HEADS * V_DIM), BF16),
        scratch_shapes=[pltpu.VMEM((tq, tk), F32), pltpu.VMEM((tq, tk), BF16), pltpu.VMEM((tq, LANES), F32),
                        pltpu.VMEM((tq, V_DIM + LANES), F32)],
        compiler_params=_params(("parallel", "parallel", "arbitrary"), vmem),
        name="attention",
    )(q, kn, kr, v)


def _merge_kernel(h_ref, bp_ref, bc_ref, bs_ref, ba_ref, g0_ref, g1_ref, g2_ref, g3_ref,
                  pp_ref, pc_ref, ps_ref, pa_ref, o_ref):
    h = h_ref[...]
    acc = None
    for b_ref, g_ref, p_ref in ((bp_ref, g0_ref, pp_ref), (bc_ref, g1_ref, pc_ref),
                                (bs_ref, g2_ref, ps_ref), (ba_ref, g3_ref, pa_ref)):
        term = jax.nn.sigmoid(_dot(h, g_ref[...])) * _dot(b_ref[...], p_ref[...])
        acc = term if acc is None else acc + term
    o_ref[...] = acc.astype(BF16)


def _merge(h, branches, w_gate, projs, l, tm, tn):
    m = h.shape[0]
    nj = D_MODEL // tn
    row = lambda i, j: (i, 0)
    in_specs = [pl.BlockSpec((tm, D_MODEL), row)]
    in_specs += [pl.BlockSpec((tm, b.shape[1]), row) for b in branches]
    in_specs += [pl.BlockSpec((None, D_MODEL, tn), functools.partial(lambda i, j, b: (l, 0, b * nj + j), b=b))
                 for b in range(N_BRANCH)]
    in_specs += [pl.BlockSpec((None, p.shape[1], tn), lambda i, j: (l, 0, j)) for p in projs]
    kin = sum(b.shape[1] for b in branches)
    vmem = 2 * 2 * (tm * D_MODEL + tm * kin + N_BRANCH * D_MODEL * tn + kin * tn + tm * tn) + 6 * tm * tn * 4
    return pl.pallas_call(
        _merge_kernel,
        grid=(m // tm, nj),
        in_specs=in_specs,
        out_specs=pl.BlockSpec((tm, tn), lambda i, j: (i, j)),
        out_shape=jax.ShapeDtypeStruct((m, D_MODEL), BF16),
        compiler_params=_params(("parallel", "arbitrary"), vmem),
        name="merge",
    )(h, *branches, w_gate, w_gate, w_gate, w_gate, *projs)


def _outproj_kernel(m_ref, w_ref, x_ref, gpost_ref, gpre_ref, x1_ref, h2_ref, y_ref):
    mg = m_ref[...]
    for c in range(0, D_MODEL, 512):
        y_ref[:, c:c + 512] = _dot(mg, w_ref[:, c:c + 512])
    x1 = x_ref[...] + _rms(y_ref[...], gpost_ref[...])
    x1_ref[...] = x1
    h2_ref[...] = _rms(x1, gpre_ref[...]).astype(BF16)


def _outproj(merged, w_out, l, x, g_post, g_pre, tm):
    m = x.shape[0]
    row = lambda i: (i, 0)
    vec = _resident((1, D_MODEL))
    vmem = 2 * tm * D_MODEL * (2 + 4 + 4 + 2) + D_MODEL * D_MODEL * 2 + 4 * tm * D_MODEL * 4
    return pl.pallas_call(
        _outproj_kernel,
        grid=(m // tm,),
        in_specs=[pl.BlockSpec((tm, D_MODEL), row), _layer_resident((D_MODEL, D_MODEL), l),
                  pl.BlockSpec((tm, D_MODEL), row), vec, vec],
        out_specs=[pl.BlockSpec((tm, D_MODEL), row), pl.BlockSpec((tm, D_MODEL), row)],
        out_shape=[jax.ShapeDtypeStruct((m, D_MODEL), F32), jax.ShapeDtypeStruct((m, D_MODEL), BF16)],
        scratch_shapes=[pltpu.VMEM((tm, D_MODEL), F32)],
        compiler_params=_params(("parallel",), vmem),
        name="outproj",
    )(merged, w_out, x, g_post, g_pre)


_MLP_COLS = 512


def _mlp_kernel(h_ref, wu_ref, wd_ref, xs_ref, g_ref, o_ref, a_ref, x_ref):
    f = pl.program_id(1)
    tf = a_ref.shape[1]
    rows = xs_ref.shape[0]
    x_ref[pl.ds(pl.multiple_of(f * rows, rows), rows), :] = xs_ref[...]

    @pl.when(f == 0)
    def _():
        o_ref[...] = jnp.zeros_like(o_ref)

    h = h_ref[...]
    for c in range(0, tf, _MLP_COLS):
        up = _dot(h, wu_ref[:, c:c + _MLP_COLS])
        a_ref[:, c:c + _MLP_COLS] = jnp.square(jnp.maximum(up, 0.0)).astype(BF16)
    a = a_ref[...]
    for c in range(0, D_MODEL, _MLP_COLS):
        o_ref[:, c:c + _MLP_COLS] += _dot(a, wd_ref[:, c:c + _MLP_COLS])

    @pl.when(f == pl.num_programs(1) - 1)
    def _():
        o_ref[...] = x_ref[...] + _rms(o_ref[...], g_ref[...])


def _mlp(h2, w_up, w_down, l, x1, g_post, tm, tf):
    m = x1.shape[0]
    row = lambda i, f: (i, 0)
    nf = D_FF // tf
    rows = tm // nf
    assert tm % nf == 0 and rows % SUBLANES == 0
    vmem = (tm * D_MODEL * (2 * 2 + 4 + 2 * 4) + 2 * 2 * 2 * D_MODEL * tf + tm * tf * 2
            + 3 * tm * _MLP_COLS * 4 + 4 * 1024 * 1024)
    return pl.pallas_call(
        _mlp_kernel,
        grid=(m // tm, nf),
        in_specs=[pl.BlockSpec((tm, D_MODEL), row), pl.BlockSpec((None, D_MODEL, tf), lambda i, f: (l, 0, f)),
                  pl.BlockSpec((None, tf, D_MODEL), lambda i, f: (l, f, 0)),
                  pl.BlockSpec((rows, D_MODEL), lambda i, f: (i * nf + f, 0)),
                  _resident((1, D_MODEL))],
        out_specs=pl.BlockSpec((tm, D_MODEL), row),
        out_shape=jax.ShapeDtypeStruct((m, D_MODEL), F32),
        scratch_shapes=[pltpu.VMEM((tm, tf), BF16), pltpu.VMEM((tm, D_MODEL), F32)],
        compiler_params=_params(("parallel", "arbitrary"), vmem),
        name="mlp",
    )(h2, w_up, w_down, x1, g_post)


def _small_in_weight(w_in):
    depth = w_in.shape[0]
    pad = jnp.zeros((depth, D_MODEL, KR_PAD - QK_ROPE), BF16)
    return jnp.concatenate([w_in[:, :, :OFF_GATE].astype(BF16), pad], axis=2)


def _q_weight(w_uq):
    depth = w_uq.shape[0]
    w = w_uq.astype(BF16).reshape(depth, Q_LORA, MLA_HEADS, QK_NOPE + QK_ROPE)
    pad = jnp.zeros((depth, Q_LORA, MLA_HEADS, Q_HEAD_PAD - QK_NOPE - QK_ROPE), BF16)
    return jnp.concatenate([w, pad], axis=3).reshape(depth, Q_LORA, MLA_HEADS * Q_HEAD_PAD)


def _kv_weight(w_ukv):
    depth = w_ukv.shape[0]
    w = w_ukv.astype(BF16).reshape(depth, KV_LORA, MLA_HEADS, QK_NOPE + V_DIM)
    k = w[..., :QK_NOPE].reshape(depth, KV_LORA, MLA_HEADS * QK_NOPE)
    v = w[..., QK_NOPE:].reshape(depth, KV_LORA, MLA_HEADS * V_DIM)
    return jnp.concatenate([k, v], axis=2)


def _tile(n, want):
    t = min(n, want)
    assert n % t == 0, (n, t)
    return t


def kernel(x, positions, pre_mix_g, w_in, pool_w, pool_scale, pool_proj, conv_w, conv_b, conv_norm_g, conv_norm_b, conv_proj, sgu_norm_g, sgu_norm_b, sgu_w, sgu_b, sgu_proj, q_norm_g, w_uq, kv_norm_g, w_ukv, attn_proj, w_out, post_mix_g, pre_mlp_g, w_up, w_down, post_mlp_g):
    batch, seq, d = x.shape
    assert d == D_MODEL and seq % CHUNK == 0
    m = batch * seq
    depth = w_in.shape[0]
    ts = _tile(seq, 512)
    ts_conv = _tile(seq, 256)
    tq = _tile(seq, 4096)
    tk = _tile(tq, 512)
    tm = _tile(m, 512)
    tm_mlp = _tile(m, 1024)

    xf = x.reshape(m, D_MODEL)
    tables = _rope_tables(positions.reshape(m, 1), ts)
    row = lambda v: v[None, :]

    w_small = _small_in_weight(w_in)
    w_gate = w_in[:, :, OFF_GATE:].astype(BF16)
    pool_wb = pool_w.astype(BF16)
    projs = tuple(p.astype(BF16) for p in (pool_proj, conv_proj, sgu_proj, attn_proj))
    wq, wkv = _q_weight(w_uq), _kv_weight(w_ukv)
    w_outb, w_upb, w_downb = w_out.astype(BF16), w_up.astype(BF16), w_down.astype(BF16)

    for l in range(depth):
        h, z_pool, z_conv, z_sgu, cq, ckv, kr = _inproj(xf, row(pre_mix_g[l]), w_small, l, tm)
        y_pool = _pool(z_pool, pool_wb, l, row(pool_scale[l]), ts, seq)
        y_conv = _conv(z_conv, conv_w[l], row(conv_b[l]), row(conv_norm_g[l]), row(conv_norm_b[l]), ts_conv, seq)
        y_sgu = _sgu(z_sgu, row(sgu_norm_g[l]), row(sgu_norm_b[l]), sgu_w[l], sgu_b[l].T, ts)
        q, kn, kro, v = _mla_prep(cq, ckv, kr, tables, row(q_norm_g[l]), row(kv_norm_g[l]), wq, wkv, l, ts)
        y_attn = _attention(q, kn, kro, v, batch, seq, tq, tk)
        merged = _merge(h, (y_pool, y_conv, y_sgu, y_attn), w_gate, projs, l, tm_mlp, 512)
        x1, h2 = _outproj(merged, w_outb, l, xf, row(post_mix_g[l]), row(pre_mlp_g[l]), tm)
        xf = _mlp(h2, w_upb, w_downb, l, x1, row(post_mlp_g[l]), tm_mlp, 512)
    return xf.reshape(batch, seq, D_MODEL)
```

```python
import functools

import jax
import jax.numpy as jnp
from jax import lax
from jax.experimental import pallas as pl
from jax.experimental.pallas import tpu as pltpu

F32 = jnp.float32
BF16 = jnp.bfloat16

D_MODEL = 2048
EPS = 1e-6
N_BRANCH = 4
D_FF = 4 * D_MODEL
POOL_DIM = 512
POOL_WINDOWS = (2, 4, 8, 16)
POOL_GDIM = POOL_DIM // len(POOL_WINDOWS)
CONV_DIM = 512
CONV_WIDTH = 31
SGU_DIM = 512
SGU_GROUPS = 4
SGU_GDIM = SGU_DIM // SGU_GROUPS
CHUNK = 128
MLA_HEADS = 8
Q_LORA = 512
KV_LORA = 512
QK_NOPE = 128
QK_ROPE = 64
V_DIM = 128
ROPE_THETA = 10000.0
OFF_KR = POOL_DIM + 2 * CONV_DIM + 2 * SGU_DIM + Q_LORA + KV_LORA
OFF_GATE = OFF_KR + QK_ROPE

LANES = 128
SUBLANES = 8
Q_HEAD_PAD = 2 * LANES
KR_PAD = LANES
N_SMALL = OFF_KR + KR_PAD
POOL_HALO = 32
CONV_HALO = 32
NEG_BIG = -1e30
LOG2_E = 1.4426950408889634
VMEM_CAP = 56 * 1024 * 1024

_SEGS = (POOL_DIM, 2 * CONV_DIM, 2 * SGU_DIM, Q_LORA, KV_LORA, KR_PAD)


def _params(semantics, vmem_bytes):
    return pltpu.CompilerParams(dimension_semantics=semantics,
                                vmem_limit_bytes=int(min(VMEM_CAP, max(vmem_bytes, 16 * 1024 * 1024))))


def _resident(shape):
    nd = len(shape)
    return pl.BlockSpec(shape, lambda *_: (0,) * nd, pipeline_mode=pl.Buffered(1))


def _layer_resident(shape, l):
    nd = len(shape)
    return pl.BlockSpec((None,) + tuple(shape), lambda *_: (l,) + (0,) * nd, pipeline_mode=pl.Buffered(1))


def _rms(x, g):
    return x * lax.rsqrt(jnp.mean(x * x, axis=-1, keepdims=True) + EPS) * g


def _layernorm(x, g, b):
    mu = jnp.mean(x, axis=-1, keepdims=True)
    xc = x - mu
    var = jnp.mean(xc * xc, axis=-1, keepdims=True)
    return xc * lax.rsqrt(var + EPS) * g + b


def _dot(a, b):
    return jnp.dot(a, b, preferred_element_type=F32)


def _rope_table_kernel(pos_ref, freq_ref, c_ref, s1_ref, s2_ref):
    ang = pos_ref[...].astype(F32) * freq_ref[...]
    lane = lax.broadcasted_iota(jnp.int32, ang.shape, 1)
    cos = jnp.cos(ang)
    sin = jnp.sin(ang)
    half = QK_ROPE // 2
    c_ref[...] = jnp.where(lane < QK_ROPE, cos, 0.0)
    s1_ref[...] = jnp.where(lane < half, 0.0, jnp.where(lane < QK_ROPE, sin, 0.0))
    s2_ref[...] = jnp.where(lane < half, -sin, 0.0)


def _rope_tables(pos, tm):
    m = pos.shape[0]
    inv_freq = ROPE_THETA ** (-jnp.arange(0, QK_ROPE, 2, dtype=F32) / QK_ROPE)
    freq = jnp.concatenate([inv_freq, inv_freq, jnp.zeros((LANES - QK_ROPE,), F32)])[None, :]
    out = jax.ShapeDtypeStruct((m, LANES), F32)
    spec = pl.BlockSpec((tm, LANES), lambda i: (i, 0))
    return pl.pallas_call(
        _rope_table_kernel,
        grid=(m // tm,),
        in_specs=[pl.BlockSpec((tm, 1), lambda i: (i, 0)), pl.BlockSpec((1, LANES), lambda i: (0, 0))],
        out_specs=[spec, spec, spec],
        out_shape=[out, out, out],
        compiler_params=_params(("parallel",), 0),
        name="rope_tables",
    )(pos, freq)


def _rope(chunk, c, s1, s2):
    half = QK_ROPE // 2
    return (chunk * c + pltpu.roll(chunk, half, 1) * s1
            + pltpu.roll(chunk, LANES - half, 1) * s2)


def _inproj_kernel(x_ref, g_ref, w_ref, h_ref, *z_refs):
    h = _rms(x_ref[...], g_ref[...]).astype(BF16)
    h_ref[...] = h
    off = 0
    for z_ref, width in zip(z_refs, _SEGS):
        for c in range(0, width, 512):
            cw = min(512, width - c)
            z_ref[:, c:c + cw] = _dot(h, w_ref[:, off + c:off + c + cw]).astype(BF16)
        off += width


def _inproj(x, g, w_small, l, tm):
    m = x.shape[0]
    row = lambda i: (i, 0)
    out_shape = [jax.ShapeDtypeStruct((m, D_MODEL), BF16)]
    out_specs = [pl.BlockSpec((tm, D_MODEL), row)]
    for width in _SEGS:
        out_shape.append(jax.ShapeDtypeStruct((m, width), BF16))
        out_specs.append(pl.BlockSpec((tm, width), row))
    vmem = (2 * tm * D_MODEL * 4 + 2 * tm * D_MODEL * 2 + D_MODEL * N_SMALL * 2
            + 2 * tm * N_SMALL * 2 + 4 * tm * D_MODEL * 4)
    return pl.pallas_call(
        _inproj_kernel,
        grid=(m // tm,),
        in_specs=[pl.BlockSpec((tm, D_MODEL), row), _resident((1, D_MODEL)),
                  _layer_resident((D_MODEL, N_SMALL), l)],
        out_specs=out_specs,
        out_shape=out_shape,
        compiler_params=_params(("parallel",), vmem),
        name="inproj",
    )(x, g, w_small)


def _pool_kernel(a_ref, halo_ref, w_ref, scale_ref, o_ref, ext_ref, *, ts, tiles_per_seq):
    i = pl.program_id(0)
    first = (i % tiles_per_seq) == 0
    a = a_ref[...].astype(F32)
    ext_ref[0:POOL_HALO, :] = jnp.where(first, 0.0, halo_ref[...].astype(F32))
    ext_ref[POOL_HALO:, :] = a
    t = (i % tiles_per_seq) * ts + lax.broadcasted_iota(jnp.int32, (ts, 1), 0)
    lo, hi = POOL_HALO // 2, POOL_HALO + ts
    for gi, w in enumerate(POOL_WINDOWS):
        c0, c1 = gi * POOL_GDIM, (gi + 1) * POOL_GDIM
        s = 1
        while s < w:
            ext_ref[lo:hi, c0:c1] = ext_ref[lo:hi, c0:c1] + ext_ref[lo - s:hi - s, c0:c1]
            s *= 2
        win = ext_ref[POOL_HALO:hi, c0:c1]
        count = jnp.minimum(t + 1, w).astype(F32)
        pooled = (win / count - a[:, c0:c1]).astype(BF16)
        o_ref[:, c0:c1] = (_dot(pooled, w_ref[gi]) * scale_ref[:, c0:c1]).astype(BF16)


def _pool(z_pool, pool_w, l, pool_scale, ts, seq):
    m = z_pool.shape[0]
    hb = ts // POOL_HALO
    kern = functools.partial(_pool_kernel, ts=ts, tiles_per_seq=seq // ts)
    return pl.pallas_call(
        kern,
        grid=(m // ts,),
        in_specs=[pl.BlockSpec((ts, POOL_DIM), lambda i: (i, 0)),
                  pl.BlockSpec((POOL_HALO, POOL_DIM), lambda i: (jnp.maximum(i * hb - 1, 0), 0)),
                  _layer_resident(pool_w.shape[1:], l), _resident((1, POOL_DIM))],
        out_specs=pl.BlockSpec((ts, POOL_DIM), lambda i: (i, 0)),
        out_shape=jax.ShapeDtypeStruct((m, POOL_DIM), BF16),
        scratch_shapes=[pltpu.VMEM((ts + POOL_HALO, POOL_DIM), F32)],
        compiler_params=_params(("parallel",), 0),
        name="pool",
    )(z_pool, z_pool, pool_w, pool_scale)


_CONV_ROWS = 64


def _glu(c):
    c = c.astype(F32)
    return c[:, :CONV_DIM] * jax.nn.sigmoid(c[:, CONV_DIM:])


_CONV_PROJ_ROWS = 256


def _conv_kernel(c_ref, halo_ref, w_ref, b_ref, ng_ref, nb_ref, proj_ref, yp_ref, pproj_ref, ys_ref, sproj_ref,
                 o_ref, op_ref, os_ref, ext_ref, y_ref, *, ts, tiles_per_seq):
    i = pl.program_id(0)
    first = (i % tiles_per_seq) == 0
    ext_ref[0, 0:CONV_HALO, :] = jnp.where(first, 0.0, _glu(halo_ref[...]))
    ext_ref[0, CONV_HALO:, :] = _glu(c_ref[...])
    span = ts + CONV_HALO - SUBLANES
    for p in range(1, SUBLANES):
        ext_ref[p, 0:span, :] = ext_ref[0, p:p + span, :]
    base = CONV_HALO - (CONV_WIDTH - 1)
    def project(src, p_ref, dst_ref, rows):
        for n in range(0, D_MODEL, 512):
            dst_ref[rows, n:n + 512] = _dot(src, p_ref[:, n:n + 512]).astype(BF16)

    for r0 in range(0, ts, _CONV_PROJ_ROWS):
        rows = slice(r0, r0 + _CONV_PROJ_ROWS)
        project(yp_ref[rows, :], pproj_ref, op_ref, rows)
        project(ys_ref[rows, :], sproj_ref, os_ref, rows)
        for r in range(r0, r0 + _CONV_PROJ_ROWS, _CONV_ROWS):
            for c in range(0, CONV_DIM, LANES):
                acc = None
                for k in range(CONV_WIDTH):
                    a, p = divmod(base + k, SUBLANES)
                    u = SUBLANES * a + r
                    term = ext_ref[p, u:u + _CONV_ROWS, c:c + LANES] * w_ref[k:k + 1, c:c + LANES]
                    acc = term if acc is None else acc + term
                y_ref[r:r + _CONV_ROWS, c:c + LANES] = acc
        y = _layernorm(y_ref[rows, :] + b_ref[...], ng_ref[...], nb_ref[...])
        project((y * jax.nn.sigmoid(y)).astype(BF16), proj_ref, o_ref, rows)


def _conv(z_conv, conv_w, conv_b, norm_g, norm_b, y_pool, y_sgu, projs, l, ts, seq):
    m = z_conv.shape[0]
    hb = ts // CONV_HALO
    kern = functools.partial(_conv_kernel, ts=ts, tiles_per_seq=seq // ts)
    vec = _resident((1, CONV_DIM))
    out = jax.ShapeDtypeStruct((m, D_MODEL), BF16)
    out_spec = pl.BlockSpec((ts, D_MODEL), lambda i: (i, 0))
    return pl.pallas_call(
        kern,
        grid=(m // ts,),
        in_specs=[pl.BlockSpec((ts, 2 * CONV_DIM), lambda i: (i, 0)),
                  pl.BlockSpec((CONV_HALO, 2 * CONV_DIM), lambda i: (jnp.maximum(i * hb - 1, 0), 0)),
                  _resident((CONV_WIDTH, CONV_DIM)), vec, vec, vec, _layer_resident((CONV_DIM, D_MODEL), l),
                  pl.BlockSpec((ts, POOL_DIM), lambda i: (i, 0)), _layer_resident((POOL_DIM, D_MODEL), l),
                  pl.BlockSpec((ts, SGU_DIM), lambda i: (i, 0)), _layer_resident((SGU_DIM, D_MODEL), l)],
        out_specs=[out_spec, out_spec, out_spec],
        out_shape=[out, out, out],
        scratch_shapes=[pltpu.VMEM((SUBLANES, ts + CONV_HALO, CONV_DIM), F32), pltpu.VMEM((ts, CONV_DIM), F32)],
        compiler_params=_params(("parallel",), (SUBLANES + 2) * (ts + CONV_HALO) * CONV_DIM * 4
                                + 3 * 2 * (CONV_DIM + 2 * ts) * D_MODEL * 2 + 8 * ts * CONV_DIM * 4),
        name="conv",
    )(z_conv, z_conv, conv_w, conv_b, norm_g, norm_b, projs[1], y_pool, projs[0], y_sgu, projs[2])


def _gelu_tanh(x):
    return 0.5 * x * (1.0 + jnp.tanh(0.7978845608028654 * (x + 0.044715 * (x * x * x))))


def _sgu_kernel(z_ref, ng_ref, nb_ref, w_ref, bt_ref, o_ref, *, ts):
    z = _gelu_tanh(z_ref[...].astype(F32))
    u = z[:, :SGU_DIM]
    v = _layernorm(z[:, SGU_DIM:], ng_ref[...], nb_ref[...]).astype(BF16)
    row = lax.broadcasted_iota(jnp.int32, (CHUNK, CHUNK), 0)
    col = lax.broadcasted_iota(jnp.int32, (CHUNK, CHUNK), 1)
    n_chunks = ts // CHUNK
    for g in range(SGU_GROUPS):
        c0, c1 = g * SGU_GDIM, (g + 1) * SGU_GDIM
        w = jnp.where(col <= row, w_ref[g], 0.0).astype(BF16)
        rhs = jnp.concatenate([v[n * CHUNK:(n + 1) * CHUNK, c0:c1] for n in range(n_chunks)], axis=1)
        sp = _dot(w, rhs) + bt_ref[:, g:g + 1]
        for n in range(n_chunks):
            o_ref[n * CHUNK:(n + 1) * CHUNK, c0:c1] = (
                u[n * CHUNK:(n + 1) * CHUNK, c0:c1] * sp[:, n * SGU_GDIM:(n + 1) * SGU_GDIM]).astype(BF16)


def _sgu(z_sgu, norm_g, norm_b, sgu_w, sgu_bt, ts):
    m = z_sgu.shape[0]
    vec = _resident((1, SGU_DIM))
    return pl.pallas_call(
        functools.partial(_sgu_kernel, ts=ts),
        grid=(m // ts,),
        in_specs=[pl.BlockSpec((ts, 2 * SGU_DIM), lambda i: (i, 0)), vec, vec,
                  _resident(sgu_w.shape), _resident(sgu_bt.shape)],
        out_specs=pl.BlockSpec((ts, SGU_DIM), lambda i: (i, 0)),
        out_shape=jax.ShapeDtypeStruct((m, SGU_DIM), BF16),
        compiler_params=_params(("parallel",), 0),
        name="sgu",
    )(z_sgu, norm_g, norm_b, sgu_w, sgu_bt)


def _mla_prep_kernel(cq_ref, ckv_ref, kr_ref, c_ref, s1_ref, s2_ref, qg_ref, kvg_ref, wq_ref, wkv_ref,
                     q_ref, kn_ref, kro_ref, v_ref):
    c, s1, s2 = c_ref[...], s1_ref[...], s2_ref[...]
    scale = (QK_NOPE + QK_ROPE) ** -0.5 * LOG2_E
    qn = _rms(cq_ref[...].astype(F32), qg_ref[...]).astype(BF16)
    for h in range(MLA_HEADS):
        o = h * Q_HEAD_PAD
        qh = _dot(qn, wq_ref[:, o:o + Q_HEAD_PAD]) * scale
        q_ref[:, o:o + LANES] = qh[:, :LANES].astype(BF16)
        q_ref[:, o + LANES:o + Q_HEAD_PAD] = _rope(qh[:, LANES:], c, s1, s2).astype(BF16)
    kvn = _rms(ckv_ref[...].astype(F32), kvg_ref[...]).astype(BF16)
    width = MLA_HEADS * QK_NOPE
    for o in range(0, width, 512):
        kn_ref[:, o:o + 512] = _dot(kvn, wkv_ref[:, o:o + 512]).astype(BF16)
        v_ref[:, o:o + 512] = _dot(kvn, wkv_ref[:, width + o:width + o + 512]).astype(BF16)
    kro_ref[...] = _rope(kr_ref[...].astype(F32), c, s1, s2).astype(BF16)


def _mla_prep(cq, ckv, kr, tables, q_norm_g, kv_norm_g, wq, wkv, l, ts):
    m = cq.shape[0]
    row = lambda i: (i, 0)
    tab = pl.BlockSpec((ts, LANES), row)
    hq = MLA_HEADS * Q_HEAD_PAD
    hk = MLA_HEADS * QK_NOPE
    return pl.pallas_call(
        _mla_prep_kernel,
        grid=(m // ts,),
        in_specs=[pl.BlockSpec((ts, Q_LORA), row), pl.BlockSpec((ts, KV_LORA), row),
                  pl.BlockSpec((ts, KR_PAD), row), tab, tab, tab,
                  _resident((1, Q_LORA)), _resident((1, KV_LORA)),
                  _layer_resident(wq.shape[1:], l), _layer_resident(wkv.shape[1:], l)],
        out_specs=[pl.BlockSpec((ts, hq), row), pl.BlockSpec((ts, hk), row),
                   pl.BlockSpec((ts, KR_PAD), row), pl.BlockSpec((ts, hk), row)],
        out_shape=[jax.ShapeDtypeStruct((m, hq), BF16), jax.ShapeDtypeStruct((m, hk), BF16),
                   jax.ShapeDtypeStruct((m, KR_PAD), BF16), jax.ShapeDtypeStruct((m, hk), BF16)],
        compiler_params=_params(("parallel",), 32 * 1024 * 1024),
        name="mla_prep",
    )(cq, ckv, kr, *tables, q_norm_g, kv_norm_g, wq, wkv)


_ATTN_ROWS = 64


def _attn_kernel(q_ref, kn_ref, kr_ref, v_ref, o_ref, s_ref, p_ref, m_ref, acc_ref, *, tq, tk):
    qi = pl.program_id(2)
    n_diag = tq // tk
    m_ref[...] = jnp.full(m_ref.shape, NEG_BIG, F32)
    acc_ref[...] = jnp.zeros(acc_ref.shape, F32)
    ones = jnp.ones((tk, LANES), BF16)

    def block(j, row_start, mask_shift):
        k0 = pl.multiple_of(j * tk, tk)
        k = jnp.concatenate([kn_ref[pl.ds(k0, tk), :], kr_ref[pl.ds(k0, tk), :]], axis=-1)
        v_ones = jnp.concatenate([v_ref[pl.ds(k0, tk), :], ones], axis=-1)
        s_ref[row_start:, :] = lax.dot_general(q_ref[row_start:, :], k, (((1,), (1,)), ((), ())),
                                               preferred_element_type=F32)
        for c in range(row_start, tq, _ATTN_ROWS):
            rows = slice(c, c + _ATTN_ROWS)
            s = s_ref[rows, :]
            if mask_shift is not None and mask_shift + tk - 1 > c:
                row = lax.broadcasted_iota(jnp.int32, s.shape, 0) + c
                col = lax.broadcasted_iota(jnp.int32, s.shape, 1) + mask_shift
                s = jnp.where(col <= row, s, NEG_BIG)
            m_prev = m_ref[rows, :]
            m_next = jnp.maximum(m_prev, jnp.max(s, axis=1, keepdims=True))
            alpha = jnp.exp2(m_prev - m_next)
            p_ref[rows, :] = jnp.exp2((s - jnp.tile(m_next, (1, tk // LANES))).astype(BF16))
            m_ref[rows, :] = m_next
            acc_ref[rows, :] = jnp.tile(alpha, (1, 2)) * acc_ref[rows, :]
        acc_ref[row_start:, :] += _dot(p_ref[row_start:, :], v_ones)

    def body(j, carry):
        block(j, 0, None)
        return carry

    lax.fori_loop(0, qi * n_diag, body, 0)
    for d in range(n_diag):
        block(qi * n_diag + d, d * tk, d * tk)
    o_ref[...] = (acc_ref[:, :V_DIM] / acc_ref[:, V_DIM:]).astype(BF16)


def _attention(q, kn, kr, v, batch, seq, tq, tk):
    m = q.shape[0]
    nq = seq // tq
    assert tq % tk == 0 and tk % LANES == 0 and tk % _ATTN_ROWS == 0
    kern = functools.partial(_attn_kernel, tq=tq, tk=tk)
    vmem = (2 * 2 * (tq * (Q_HEAD_PAD + V_DIM) + seq * (QK_NOPE + KR_PAD + V_DIM))
            + tq * tk * (4 + 2) + tq * (2 * LANES + V_DIM) * 4 + 12 * 1024 * 1024)
    return pl.pallas_call(
        kern,
        grid=(batch, MLA_HEADS, nq),
        in_specs=[pl.BlockSpec((tq, Q_HEAD_PAD), lambda b, h, i: (b * nq + i, h)),
                  pl.BlockSpec((seq, QK_NOPE), lambda b, h, i: (b, h)),
                  pl.BlockSpec((seq, KR_PAD), lambda b, h, i: (b, 0)),
                  pl.BlockSpec((seq, V_DIM), lambda b, h, i: (b, h))],
        out_specs=pl.BlockSpec((tq, V_DIM), lambda b, h, i: (b * nq + i, h)),
        out_shape=jax.ShapeDtypeStruct((m, MLA_HEADS * V_DIM), BF16),
        scratch_shapes=[pltpu.VMEM((tq, tk), F32), pltpu.VMEM((tq, tk), BF16), pltpu.VMEM((tq, LANES), F32),
                        pltpu.VMEM((tq, V_DIM + LANES), F32)],
        compiler_params=_params(("parallel", "parallel", "arbitrary"), vmem),
        name="attention",
    )(q, kn, kr, v)


_MERGE_COLS = 256


def _merge_kernel(*refs, projected):
    h_ref, b_refs, g_refs = refs[0], refs[1:1 + N_BRANCH], refs[1 + N_BRANCH:1 + 2 * N_BRANCH]
    p_refs, o_ref = iter(refs[1 + 2 * N_BRANCH:-1]), refs[-1]
    h = h_ref[...]
    p_refs = [None if done else next(p_refs) for done in projected]
    for c in range(0, o_ref.shape[1], _MERGE_COLS):
        cols = slice(c, c + _MERGE_COLS)
        acc = None
        for b_ref, g_ref, p_ref in zip(b_refs, g_refs, p_refs):
            y = b_ref[:, cols].astype(F32) if p_ref is None else _dot(b_ref[...], p_ref[:, cols])
            term = jax.nn.sigmoid(_dot(h, g_ref[:, cols])) * y
            acc = term if acc is None else acc + term
        o_ref[:, cols] = acc.astype(BF16)


def _merge(h, branches, w_gate, projs, l, tm, tn):
    m = h.shape[0]
    nj = D_MODEL // tn
    row = lambda i, j: (i, 0)
    projected = tuple(p is None for p in projs)
    in_specs = [pl.BlockSpec((tm, D_MODEL), row)]
    in_specs += [pl.BlockSpec((tm, tn), lambda i, j: (i, j)) if done else pl.BlockSpec((tm, b.shape[1]), row)
                 for b, done in zip(branches, projected)]
    in_specs += [pl.BlockSpec((None, D_MODEL, tn), functools.partial(lambda i, j, b: (l, 0, b * nj + j), b=b))
                 for b in range(N_BRANCH)]
    todo = [p for p in projs if p is not None]
    in_specs += [pl.BlockSpec((None, p.shape[1], tn), lambda i, j: (l, 0, j)) for p in todo]
    kin = sum(p.shape[1] for p in todo)
    n_done = sum(projected)
    vmem = (2 * 2 * (tm * D_MODEL + tm * kin + N_BRANCH * D_MODEL * tn + kin * tn + (1 + n_done) * tm * tn)
            + 6 * tm * tn * 4)
    return pl.pallas_call(
        functools.partial(_merge_kernel, projected=projected),
        grid=(m // tm, nj),
        in_specs=in_specs,
        out_specs=pl.BlockSpec((tm, tn), lambda i, j: (i, j)),
        out_shape=jax.ShapeDtypeStruct((m, D_MODEL), BF16),
        compiler_params=_params(("parallel", "arbitrary"), vmem),
        name="merge",
    )(h, *branches, w_gate, w_gate, w_gate, w_gate, *todo)


def _outproj_kernel(m_ref, w_ref, x_ref, gpost_ref, gpre_ref, x1_ref, h2_ref, y_ref):
    mg = m_ref[...]
    for c in range(0, D_MODEL, 512):
        y_ref[:, c:c + 512] = _dot(mg, w_ref[:, c:c + 512])
    x1 = x_ref[...] + _rms(y_ref[...], gpost_ref[...])
    x1_ref[...] = x1
    h2_ref[...] = _rms(x1, gpre_ref[...]).astype(BF16)


def _outproj(merged, w_out, l, x, g_post, g_pre, tm):
    m = x.shape[0]
    row = lambda i: (i, 0)
    vec = _resident((1, D_MODEL))
    vmem = 2 * tm * D_MODEL * (2 + 4 + 4 + 2) + D_MODEL * D_MODEL * 2 + 4 * tm * D_MODEL * 4
    return pl.pallas_call(
        _outproj_kernel,
        grid=(m // tm,),
        in_specs=[pl.BlockSpec((tm, D_MODEL), row), _layer_resident((D_MODEL, D_MODEL), l),
                  pl.BlockSpec((tm, D_MODEL), row), vec, vec],
        out_specs=[pl.BlockSpec((tm, D_MODEL), row), pl.BlockSpec((tm, D_MODEL), row)],
        out_shape=[jax.ShapeDtypeStruct((m, D_MODEL), F32), jax.ShapeDtypeStruct((m, D_MODEL), BF16)],
        scratch_shapes=[pltpu.VMEM((tm, D_MODEL), F32)],
        compiler_params=_params(("parallel",), vmem),
        name="outproj",
    )(merged, w_out, x, g_post, g_pre)


_MLP_COLS = 512
_MLP_NORM_ROWS = 16


def _mlp_kernel(h_ref, wu_ref, wd_ref, xs_ref, g_ref, o_ref, a_ref, x_ref):
    f = pl.program_id(1)
    tf = a_ref.shape[1]
    rows = xs_ref.shape[0]
    x_ref[pl.ds(pl.multiple_of(f * rows, rows), rows), :] = xs_ref[...]

    @pl.when(f == 0)
    def _():
        o_ref[...] = jnp.zeros_like(o_ref)

    h = h_ref[...]
    for c in range(0, tf, _MLP_COLS):
        up = _dot(h, wu_ref[:, c:c + _MLP_COLS])
        a_ref[:, c:c + _MLP_COLS] = jnp.square(jnp.maximum(up, 0.0)).astype(BF16)
    a = a_ref[...]
    for c in range(0, D_MODEL, _MLP_COLS):
        o_ref[:, c:c + _MLP_COLS] += _dot(a, wd_ref[:, c:c + _MLP_COLS])

    @pl.when(f == pl.num_programs(1) - 1)
    def _():
        for r in range(0, o_ref.shape[0], _MLP_NORM_ROWS):
            rows = slice(r, r + _MLP_NORM_ROWS)
            o_ref[rows, :] = x_ref[rows, :] + _rms(o_ref[rows, :], g_ref[...])


def _mlp(h2, w_up, w_down, l, x1, g_post, tm, tf):
    m = x1.shape[0]
    row = lambda i, f: (i, 0)
    nf = D_FF // tf
    rows = tm // nf
    assert tm % nf == 0 and rows % SUBLANES == 0
    vmem = (tm * D_MODEL * (2 * 2 + 4 + 2 * 4) + 2 * 2 * 2 * D_MODEL * tf + tm * tf * 2
            + 3 * tm * _MLP_COLS * 4 + 4 * 1024 * 1024)
    return pl.pallas_call(
        _mlp_kernel,
        grid=(m // tm, nf),
        in_specs=[pl.BlockSpec((tm, D_MODEL), row), pl.BlockSpec((None, D_MODEL, tf), lambda i, f: (l, 0, f)),
                  pl.BlockSpec((None, tf, D_MODEL), lambda i, f: (l, f, 0)),
                  pl.BlockSpec((rows, D_MODEL), lambda i, f: (i * nf + f, 0)),
                  _resident((1, D_MODEL))],
        out_specs=pl.BlockSpec((tm, D_MODEL), row),
        out_shape=jax.ShapeDtypeStruct((m, D_MODEL), F32),
        scratch_shapes=[pltpu.VMEM((tm, tf), BF16), pltpu.VMEM((tm, D_MODEL), F32)],
        compiler_params=_params(("parallel", "arbitrary"), vmem),
        name="mlp",
    )(h2, w_up, w_down, x1, g_post)


def _small_in_weight(w_in):
    depth = w_in.shape[0]
    pad = jnp.zeros((depth, D_MODEL, KR_PAD - QK_ROPE), BF16)
    return jnp.concatenate([w_in[:, :, :OFF_GATE].astype(BF16), pad], axis=2)


def _q_weight(w_uq):
    depth = w_uq.shape[0]
    w = w_uq.astype(BF16).reshape(depth, Q_LORA, MLA_HEADS, QK_NOPE + QK_ROPE)
    pad = jnp.zeros((depth, Q_LORA, MLA_HEADS, Q_HEAD_PAD - QK_NOPE - QK_ROPE), BF16)
    return jnp.concatenate([w, pad], axis=3).reshape(depth, Q_LORA, MLA_HEADS * Q_HEAD_PAD)


def _kv_weight(w_ukv):
    depth = w_ukv.shape[0]
    w = w_ukv.astype(BF16).reshape(depth, KV_LORA, MLA_HEADS, QK_NOPE + V_DIM)
    k = w[..., :QK_NOPE].reshape(depth, KV_LORA, MLA_HEADS * QK_NOPE)
    v = w[..., QK_NOPE:].reshape(depth, KV_LORA, MLA_HEADS * V_DIM)
    return jnp.concatenate([k, v], axis=2)


def _tile(n, want):
    t = min(n, want)
    assert n % t == 0, (n, t)
    return t


def kernel(x, positions, pre_mix_g, w_in, pool_w, pool_scale, pool_proj, conv_w, conv_b, conv_norm_g, conv_norm_b, conv_proj, sgu_norm_g, sgu_norm_b, sgu_w, sgu_b, sgu_proj, q_norm_g, w_uq, kv_norm_g, w_ukv, attn_proj, w_out, post_mix_g, pre_mlp_g, w_up, w_down, post_mlp_g):
    batch, seq, d = x.shape
    assert d == D_MODEL and seq % CHUNK == 0
    m = batch * seq
    depth = w_in.shape[0]
    ts = _tile(seq, 512)
    ts_conv = _tile(seq, 512)
    tq = _tile(seq, 4096)
    tk = _tile(tq, 512)
    tm = _tile(m, 512)
    tm_mlp = _tile(m, 1024)

    xf = x.reshape(m, D_MODEL)
    tables = _rope_tables(positions.reshape(m, 1), ts)
    row = lambda v: v[None, :]

    w_small = _small_in_weight(w_in)
    w_gate = w_in[:, :, OFF_GATE:].astype(BF16)
    pool_wb = pool_w.astype(BF16)
    projs = tuple(p.astype(BF16) for p in (pool_proj, conv_proj, sgu_proj, attn_proj))
    wq, wkv = _q_weight(w_uq), _kv_weight(w_ukv)
    w_outb, w_upb, w_downb = w_out.astype(BF16), w_up.astype(BF16), w_down.astype(BF16)

    for l in range(depth):
        h, z_pool, z_conv, z_sgu, cq, ckv, kr = _inproj(xf, row(pre_mix_g[l]), w_small, l, tm)
        b_pool = _pool(z_pool, pool_wb, l, row(pool_scale[l]), ts, seq)
        b_sgu = _sgu(z_sgu, row(sgu_norm_g[l]), row(sgu_norm_b[l]), sgu_w[l], sgu_b[l].T, ts)
        y_conv, y_pool, y_sgu = _conv(z_conv, conv_w[l], row(conv_b[l]), row(conv_norm_g[l]), row(conv_norm_b[l]),
                                      b_pool, b_sgu, projs, l, ts_conv, seq)
        q, kn, kro, v = _mla_prep(cq, ckv, kr, tables, row(q_norm_g[l]), row(kv_norm_g[l]), wq, wkv, l, ts)
        y_attn = _attention(q, kn, kro, v, batch, seq, tq, tk)
        merged = _merge(h, (y_pool, y_conv, y_sgu, y_attn), w_gate, (None, None, None, projs[3]), l, tm_mlp, 512)
        x1, h2 = _outproj(merged, w_outb, l, xf, row(post_mix_g[l]), row(pre_mlp_g[l]), tm)
        xf = _mlp(h2, w_upb, w_downb, l, x1, row(post_mlp_g[l]), tm_mlp, 512)
    return xf.reshape(batch, seq, D_MODEL)
```
